```python
import jax
import jax.numpy as jnp
from jax import lax
import numpy as np

D_MODEL = 1024
BATCH = 16
SEQ = 2048
DEPTH = 2

GRID_W = 64
CTX_LEN = 256
D_MIX = D_MODEL
HEAD_DIM = 64
RW_W = D_MIX // 4
RW_H = RW_W // HEAD_DIM
RW_DH = HEAD_DIM
RW_LW = 64
RW_LA = 64
RW_LG = 128
NA_W = D_MIX // 2
NA_H = NA_W // HEAD_DIM
NA_KR = 8
NA_KC = 16
SC_W = D_MIX - RW_W - NA_W
SC_K = 3
PEER_H = 8
PEER_NKEYS = 128
PEER_N = PEER_NKEYS * PEER_NKEYS
PEER_DQ = 256
PEER_TOPK = 16
PEER_CHUNK = 128

NORM_EPS = 1e-6
GN_EPS = 64e-5
NEG_INF = -1e30
IN_SPLITS = (RW_W, RW_W, RW_W, 2 * RW_LW, 2 * RW_LA, RW_LG, NA_W, NA_W, NA_W, SC_W, SC_W, SC_W)
D_IN = sum(IN_SPLITS)

kernel_name = 'hybrid_rwkv7_natten_shortconv_peer_dit'


def rmsnorm(x, g):
    xf = x.astype(jnp.float32)
    y = xf * lax.rsqrt(jnp.mean(xf * xf, axis=-1, keepdims=True) + NORM_EPS)
    return (y * g.astype(jnp.float32)).astype(x.dtype)


def split_columns(z):
    offsets = [int(o) for o in np.cumsum(IN_SPLITS)[:-1]]
    return jnp.split(z, offsets, axis=-1)


def short_conv3(u, w):
    up = jnp.pad(u, ((0, 0), (1, 1), (0, 0)))
    return up[:, :-2] * w[0] + up[:, 1:-1] * w[1] + up[:, 2:] * w[2]


def gated_short_conv(z_b, z_c, z_x, w):
    return z_b * short_conv3(z_c * z_x, w)


def rwkv_prepare(z_r, z_k, z_v, z_w, z_a, w0, w_up, a0, a_up, k_k, k_a):
    f32 = jnp.float32
    B, T, _ = z_r.shape
    r, k, v = z_r.astype(f32), z_k.astype(f32), z_v.astype(f32)
    lw = jnp.tanh(z_w.astype(f32).reshape(B, T, 2, RW_LW))
    la = z_a.astype(f32).reshape(B, T, 2, RW_LA)
    w_raw = w0 + jnp.einsum('btdr,drc->btdc', lw, w_up)
    decay = jnp.exp(-jnp.exp(-jax.nn.softplus(-w_raw) - 0.5))
    a = jax.nn.sigmoid(a0 + jnp.einsum('btdr,drc->btdc', la, a_up))
    kk = (k * k_k).reshape(B, T, RW_H, RW_DH)
    kk = (kk * lax.rsqrt(jnp.sum(kk * kk, axis=-1, keepdims=True) + 1e-12)).reshape(B, T, RW_W)
    k_dir = k[:, :, None, :] * (1.0 + (a - 1.0) * k_a)
    b_dir = kk[:, :, None, :] * a
    tm = lambda t: jnp.moveaxis(t.reshape(B, T, RW_H, RW_DH), 1, 0)
    common = (tm(r), tm(kk), tm(v))
    per_dir = tuple((tm(decay[:, :, d]), tm(b_dir[:, :, d]), tm(k_dir[:, :, d])) for d in range(2))
    return common, per_dir


def wkv_scan(state0, r, kk, v, decay, b, k, reverse):
    def step(S, inp):
        r_t, kk_t, v_t, w_t, b_t, k_t = inp
        sa = jnp.einsum('bhij,bhj->bhi', S, kk_t)
        S = S * w_t[:, :, None, :] - sa[..., None] * b_t[:, :, None, :] + v_t[..., None] * k_t[:, :, None, :]
        return S, jnp.einsum('bhij,bhj->bhi', S, r_t)
    return lax.scan(step, state0, (r, kk, v, decay, b, k), reverse=reverse)


def rwkv_output(y_tm, z_r, z_k, z_v, z_g, g_up, r_k, lnx_g):
    f32 = jnp.float32
    B, T, _ = z_r.shape
    y = jnp.moveaxis(y_tm, 0, 1)
    mu = jnp.mean(y, axis=-1, keepdims=True)
    var = jnp.mean(jnp.square(y - mu), axis=-1, keepdims=True)
    y = (y - mu) * lax.rsqrt(var + GN_EPS) * lnx_g.astype(f32).reshape(RW_H, RW_DH)
    hd = lambda t: t.astype(f32).reshape(B, T, RW_H, RW_DH)
    r, k, v = hd(z_r), hd(z_k), hd(z_v)
    y = y + jnp.sum(r * k * r_k, axis=-1, keepdims=True) * v
    gate = jax.nn.sigmoid(z_g.astype(f32)) @ g_up.astype(f32)
    return (y.reshape(B, T, RW_W) * gate).astype(z_r.dtype)


def rwkv_mixer(pc, pl, w0, w_up, a0, a_up, g_up, k_k, k_a, r_k, lnx_g, need_ctx):
    c_common, c_dirs = rwkv_prepare(*pc[:5], w0, w_up, a0, a_up, k_k, k_a)
    l_common, l_dirs = rwkv_prepare(*pl[:5], w0, w_up, a0, a_up, k_k, k_a)
    state0 = jnp.zeros((pl[0].shape[0], RW_H, RW_DH, RW_DH), jnp.float32)
    ys_c, ys_l = [], []
    for d, reverse in enumerate((False, True)):
        ctx_state, y_c = wkv_scan(state0, *c_common, *c_dirs[d], reverse=reverse)
        _, y_l = wkv_scan(ctx_state, *l_common, *l_dirs[d], reverse=reverse)
        ys_c.append(y_c)
        ys_l.append(y_l)
    out_l = rwkv_output(ys_l[0] + ys_l[1], pl[0], pl[1], pl[2], pl[5], g_up, r_k, lnx_g)
    out_c = rwkv_output(ys_c[0] + ys_c[1], pc[0], pc[1], pc[2], pc[5], g_up, r_k, lnx_g) if need_ctx else None
    return out_c, out_l


def neighbourhood_attention(ql, kl, vl, kc, vc, rpb):
    f32 = jnp.float32
    B, S = ql.shape[:2]
    rows = S // GRID_W
    kr = min(NA_KR, rows)
    scale = HEAD_DIM ** -0.5
    grid = lambda t: t.reshape(B, rows, GRID_W, NA_H, HEAD_DIM)
    qg = grid(ql).astype(f32) * scale
    kg, vg = grid(kl).astype(f32), grid(vl)
    ri = jnp.arange(rows)
    row_idx = jnp.clip(ri - kr // 2, 0, rows - kr)[:, None] + jnp.arange(kr)[None, :]
    ci = jnp.arange(GRID_W)
    c0 = jnp.clip(ci - NA_KC // 2, 0, GRID_W - NA_KC)
    col_ok = (ci[None, :] >= c0[:, None]) & (ci[None, :] < c0[:, None] + NA_KC)
    dr = row_idx - ri[:, None] + (NA_KR - 1)
    dc = jnp.clip(ci[None, :] - ci[:, None] + (NA_KC - 1), 0, 2 * NA_KC - 2)
    bias = rpb[:, dr[:, None, :, None], dc[None, :, None, :]].astype(f32)
    k_band = kg[:, row_idx]
    v_band = vg[:, row_idx]
    s_win = jnp.einsum('brqhd,brkwhd->bhrqkw', qg, k_band) + bias[None]
    s_win = jnp.where(col_ok[:, None, :], s_win, NEG_INF)
    s_ctx = jnp.einsum('brqhd,bnhd->bhrqn', qg, kc.astype(f32))
    n_win = kr * GRID_W
    s = jnp.concatenate([s_win.reshape(B, NA_H, rows, GRID_W, n_win), s_ctx], axis=-1)
    p = jax.nn.softmax(s, axis=-1)
    p_win = p[..., :n_win].reshape(B, NA_H, rows, GRID_W, kr, GRID_W).astype(vl.dtype)
    p_ctx = p[..., n_win:].astype(vl.dtype)
    out = jnp.einsum('bhrqkw,brkwhd->brqhd', p_win, v_band) + jnp.einsum('bhrqn,bnhd->brqhd', p_ctx, vc)
    return out.reshape(B, S, NA_W)


def context_attention(qc, kc, vc):
    f32 = jnp.float32
    B, C = qc.shape[:2]
    s = jnp.einsum('bqhd,bkhd->bhqk', qc.astype(f32), kc.astype(f32)) * (HEAD_DIM ** -0.5)
    p = jax.nn.softmax(s, axis=-1).astype(vc.dtype)
    return jnp.einsum('bhqk,bkhd->bqhd', p, vc).reshape(B, C, NA_W)


def token_mixing(hc, hl, w_in, w_out, rw_w0, rw_w_up, rw_a0, rw_a_up, rw_g_up, rw_k_k, rw_k_a,
                 rw_r_k, rw_lnx_g, na_rpb, sc_conv_w, need_ctx):
    B, S = hl.shape[:2]
    C = hc.shape[1]
    pc = split_columns(hc @ w_in)
    pl = split_columns(hl @ w_in)
    y_rw_c, y_rw_l = rwkv_mixer(pc, pl, rw_w0, rw_w_up, rw_a0, rw_a_up, rw_g_up, rw_k_k, rw_k_a,
                                rw_r_k, rw_lnx_g, need_ctx)
    qc, kc, vc = (t.reshape(B, C, NA_H, HEAD_DIM) for t in pc[6:9])
    ql, kl, vl = (t.reshape(B, S, NA_H, HEAD_DIM) for t in pl[6:9])
    y_na_l = neighbourhood_attention(ql, kl, vl, kc, vc, na_rpb)
    y_sc_l = gated_short_conv(*pl[9:12], sc_conv_w)
    yl = jnp.concatenate([y_rw_l, y_na_l, y_sc_l], axis=-1) @ w_out
    if not need_ctx:
        return None, yl
    y_na_c = context_attention(qc, kc, vc)
    y_sc_c = gated_short_conv(*pc[9:12], sc_conv_w)
    yc = jnp.concatenate([y_rw_c, y_na_c, y_sc_c], axis=-1) @ w_out
    return yc, yl


def peer_ffn(h, q_w, sub_keys, u_tab, v_tab):
    f32 = jnp.float32
    lead = h.shape[:-1]
    t = h.reshape(-1, D_MODEL)
    n = t.shape[0]
    q = (t @ q_w).reshape(n, PEER_H, 2, PEER_DQ // 2)
    s = jnp.einsum('thpd,hpnd->thpn', q.astype(f32), sub_keys.astype(f32))
    s1, i1 = lax.top_k(s[:, :, 0], PEER_TOPK)
    s2, i2 = lax.top_k(s[:, :, 1], PEER_TOPK)
    cand_s = (s1[..., :, None] + s2[..., None, :]).reshape(n, PEER_H, PEER_TOPK * PEER_TOPK)
    cand_i = (i1[..., :, None] * PEER_NKEYS + i2[..., None, :]).reshape(n, PEER_H, PEER_TOPK * PEER_TOPK)
    top_s, pos = lax.top_k(cand_s, PEER_TOPK)
    idx = jnp.take_along_axis(cand_i, pos, axis=-1)
    gates = jax.nn.softmax(top_s, axis=-1).astype(h.dtype)

    def chunk(args):
        x_c, i_c, g_c = args
        act = jax.nn.gelu(jnp.einsum('chkd,cd->chk', u_tab[i_c], x_c))
        return jnp.einsum('chk,chkd->cd', g_c * act, v_tab[i_c])

    nc = n // PEER_CHUNK
    out = lax.map(chunk, (t.reshape(nc, PEER_CHUNK, D_MODEL),
                          idx.reshape(nc, PEER_CHUNK, PEER_H, PEER_TOPK),
                          gates.reshape(nc, PEER_CHUNK, PEER_H, PEER_TOPK)))
    return out.reshape(lead + (D_MODEL,))


def setup_inputs(seed: int = 0) -> dict:
    key = jax.random.key(seed)
    keys = iter(jax.random.split(key, 32))
    nrm = lambda shape, std: std * jax.random.normal(next(keys), shape, jnp.float32)
    L, D = DEPTH, D_MODEL
    return {
        'x': nrm((BATCH, SEQ, D), 1.0),
        'c': nrm((BATCH, D), 1.0),
        'ctx': nrm((BATCH, CTX_LEN, D), 1.0),
        'c_ctx': nrm((D,), 1.0),
        'ada_w': nrm((L, D, 6 * D), 0.5 * D ** -0.5),
        'ada_b': nrm((L, 6 * D), 0.02),
        'norm1_g': 1.0 + nrm((L, D), 0.02),
        'norm2_g': 1.0 + nrm((L, D), 0.02),
        'w_in': nrm((L, D, D_IN), D ** -0.5),
        'rw_w0': jnp.linspace(-5.0, 1.0, RW_W, dtype=jnp.float32) + nrm((L, 2, RW_W), 0.1),
        'rw_w_up': nrm((L, 2, RW_LW, RW_W), 0.5 * RW_LW ** -0.5),
        'rw_a0': nrm((L, 2, RW_W), 0.1),
        'rw_a_up': nrm((L, 2, RW_LA, RW_W), 0.5 * RW_LA ** -0.5),
        'rw_g_up': nrm((L, RW_LG, RW_W), RW_LG ** -0.5),
        'rw_k_k': 0.85 + nrm((L, RW_W), 0.05),
        'rw_k_a': 1.0 + nrm((L, RW_W), 0.05),
        'rw_r_k': nrm((L, RW_H, RW_DH), 0.1),
        'rw_lnx_g': 1.0 + nrm((L, RW_W), 0.02),
        'na_rpb': nrm((L, NA_H, 2 * NA_KR - 1, 2 * NA_KC - 1), 0.1),
        'sc_conv_w': nrm((L, SC_K, SC_W), SC_K ** -0.5),
        'w_out': nrm((L, D_MIX, D), D_MIX ** -0.5),
        'peer_q_w': nrm((L, D, PEER_H * PEER_DQ), D ** -0.5),
        'peer_sub_keys': nrm((L, PEER_H, 2, PEER_NKEYS, PEER_DQ // 2), (PEER_DQ // 2) ** -0.5),
        'peer_u': nrm((L, PEER_N, D), D ** -0.5),
        'peer_v': nrm((L, PEER_N, D), PEER_TOPK ** -0.5),
        'final_g': 1.0 + nrm((D,), 0.02),
    }


def reference(x, c, ctx, c_ctx, ada_w, ada_b, norm1_g, norm2_g, w_in, rw_w0, rw_w_up, rw_a0,
              rw_a_up, rw_g_up, rw_k_k, rw_k_a, rw_r_k, rw_lnx_g, na_rpb, sc_conv_w, w_out,
              peer_q_w, peer_sub_keys, peer_u, peer_v, final_g):
    xl, xc = x, ctx
    for i in range(DEPTH):
        need_ctx = i < DEPTH - 1
        mod_l = jax.nn.silu(c) @ ada_w[i] + ada_b[i]
        mod_c = jax.nn.silu(c_ctx) @ ada_w[i] + ada_b[i]
        sh1, sc1, g1, sh2, sc2, g2 = jnp.split(mod_l[:, None, :], 6, axis=-1)
        csh1, csc1, cg1, csh2, csc2, cg2 = jnp.split(mod_c, 6, axis=-1)
        hl = rmsnorm(xl, norm1_g[i]) * (1.0 + sc1) + sh1
        hc = rmsnorm(xc, norm1_g[i]) * (1.0 + csc1) + csh1
        yc, yl = token_mixing(hc, hl, w_in[i], w_out[i], rw_w0[i], rw_w_up[i], rw_a0[i], rw_a_up[i],
                              rw_g_up[i], rw_k_k[i], rw_k_a[i], rw_r_k[i], rw_lnx_g[i], na_rpb[i],
                              sc_conv_w[i], need_ctx)
        xl = xl + g1 * yl
        hl2 = rmsnorm(xl, norm2_g[i]) * (1.0 + sc2) + sh2
        xl = xl + g2 * peer_ffn(hl2, peer_q_w[i], peer_sub_keys[i], peer_u[i], peer_v[i])
        if need_ctx:
            xc = xc + cg1 * yc
            hc2 = rmsnorm(xc, norm2_g[i]) * (1.0 + csc2) + csh2
            xc = xc + cg2 * peer_ffn(hc2, peer_q_w[i], peer_sub_keys[i], peer_u[i], peer_v[i])
    return rmsnorm(xl, final_g)
```

```python
import functools

import numpy as np
import jax
import jax.numpy as jnp
from jax import lax
from jax.experimental import pallas as pl
from jax.experimental.pallas import tpu as pltpu

F32 = jnp.float32
BF16 = jnp.bfloat16

HEAD_DIM = 64
GRID_W = 64
RW_H = 4
RW_W = RW_H * HEAD_DIM
RW_LW = 64
RW_LA = 64
RW_LG = 128
RW_COLS = 3 * RW_W + 2 * RW_LW + 2 * RW_LA + RW_LG
NA_H = 8
NA_W = NA_H * HEAD_DIM
NA_KR = 8
NA_KC = 16
SC_W = 256
PEER_H = 8
PEER_NKEYS = 128
PEER_TOPK = 16
NORM_EPS = 1e-6
GN_EPS = 64e-5
NEG_INF = -1e30
RW_CHUNK = 64

VMEM_LIMIT = 52 * 1024 * 1024

_CAND = [(a, b) for a in range(PEER_TOPK) for b in range(PEER_TOPK) if (a + 1) * (b + 1) <= PEER_TOPK]
_NCAND = len(_CAND)
_NCAND_PAD = -(-_NCAND // 8) * 8


def _cparams(*sem):
    return pltpu.CompilerParams(dimension_semantics=sem, vmem_limit_bytes=VMEM_LIMIT)


def _mm(a, b, passes=1, nt=False):
    dims = (((1,), (1,)), ((), ())) if nt else (((1,), (0,)), ((), ()))
    dg = functools.partial(lax.dot_general, dimension_numbers=dims, preferred_element_type=F32)
    if passes == 6:
        return dg(a.astype(F32), b.astype(F32), precision=lax.Precision.HIGHEST)
    a_hi = a.astype(BF16)
    b_hi = b.astype(BF16)
    out = dg(a_hi, b_hi)
    if passes == 3:
        a_lo = (a - a_hi.astype(F32)).astype(BF16)
        b_lo = (b - b_hi.astype(F32)).astype(BF16)
        out = out + dg(a_lo, b_hi) + dg(a_hi, b_lo)
    return out


def _sigmoid(x):
    return 1.0 / (1.0 + jnp.exp(-x))


def _norm_mod(x, g, scale, shift):
    ms = jnp.mean(x * x, axis=-1, keepdims=True)
    y = x * lax.rsqrt(ms + NORM_EPS) * g
    return y * (1.0 + scale) + shift


def _ada_kernel(c_ref, w_ref, b_ref, o_ref):
    cc = c_ref[...]
    o_ref[0] = _mm(cc * _sigmoid(cc), w_ref[0], 6) + b_ref[0]


def _ada(cc, ada_w, ada_b):
    L, D, D6 = ada_w.shape
    R = cc.shape[0]
    tn = 1536
    return pl.pallas_call(
        _ada_kernel,
        grid=(L, D6 // tn),
        in_specs=[
            pl.BlockSpec((R, D), lambda l, j: (0, 0)),
            pl.BlockSpec((1, D, tn), lambda l, j: (l, 0, j)),
            pl.BlockSpec((1, 1, tn), lambda l, j: (l, 0, j)),
        ],
        out_specs=pl.BlockSpec((1, R, tn), lambda l, j: (l, 0, j)),
        out_shape=jax.ShapeDtypeStruct((L, R, D6), F32),
        compiler_params=_cparams("parallel", "parallel"),
        name="ada",
    )(cc, ada_w, ada_b.reshape(L, 1, D6))


def _proj_in_kernel(x_ref, mod_ref, g_ref, w_ref, zrw_ref, q_ref, k_ref, v_ref, zsc_ref):
    h = _norm_mod(x_ref[0], g_ref[...], mod_ref[0, 1], mod_ref[0, 0])
    z = jnp.dot(h.astype(BF16), w_ref[...], preferred_element_type=F32)
    o = RW_COLS
    zrw_ref[0] = z[:, :o]
    q_ref[0] = (z[:, o:o + NA_W] * (HEAD_DIM ** -0.5)).astype(BF16)
    k_ref[0] = z[:, o + NA_W:o + 2 * NA_W].astype(BF16)
    v_ref[0] = z[:, o + 2 * NA_W:o + 3 * NA_W].astype(BF16)
    zsc_ref[0] = z[:, o + 3 * NA_W:]


def _proj_in(x, mod, g, w_bf):
    B, T, D = x.shape
    tm = 256
    n_out = w_bf.shape[1]
    row = lambda b, t: (b, t, 0)
    return pl.pallas_call(
        _proj_in_kernel,
        grid=(B, T // tm),
        in_specs=[
            pl.BlockSpec((1, tm, D), row),
            pl.BlockSpec((1, 6, 1, D), lambda b, t: (b, 0, 0, 0)),
            pl.BlockSpec((1, D), lambda b, t: (0, 0)),
            pl.BlockSpec((D, n_out), lambda b, t: (0, 0)),
        ],
        out_specs=[
            pl.BlockSpec((1, tm, RW_COLS), row),
            pl.BlockSpec((1, tm, NA_W), row),
            pl.BlockSpec((1, tm, NA_W), row),
            pl.BlockSpec((1, tm, NA_W), row),
            pl.BlockSpec((1, tm, 3 * SC_W), row),
        ],
        out_shape=[
            jax.ShapeDtypeStruct((B, T, RW_COLS), F32),
            jax.ShapeDtypeStruct((B, T, NA_W), BF16),
            jax.ShapeDtypeStruct((B, T, NA_W), BF16),
            jax.ShapeDtypeStruct((B, T, NA_W), BF16),
            jax.ShapeDtypeStruct((B, T, 3 * SC_W), F32),
        ],
        compiler_params=_cparams("parallel", "parallel"),
        name="proj_in",
    )(x, mod, g.reshape(1, D), w_bf)


RW_P_ALG = 3
RW_P_LORA = 3


def _rwkv_prep_kernel(z_ref, w0_ref, wup_ref, a0_ref, aup_ref, kk_ref, ka_ref,
                      phi_ref, zc_ref, wr_ref, yin_ref):
    C = z_ref.shape[1]
    W = RW_W
    z = z_ref[0]
    r = z[:, 0:W]
    k = z[:, W:2 * W]
    v = z[:, 2 * W:3 * W]
    zw = z[:, 3 * W:3 * W + 2 * RW_LW]
    za = z[:, 3 * W + 2 * RW_LW:3 * W + 2 * RW_LW + 2 * RW_LA]
    row = lax.broadcasted_iota(jnp.int32, (C, C), 0)
    col = lax.broadcasted_iota(jnp.int32, (C, C), 1)
    eye = row == col
    eye_j = (lax.broadcasted_iota(jnp.int32, (HEAD_DIM, HEAD_DIM), 0)
             == lax.broadcasted_iota(jnp.int32, (HEAD_DIM, HEAD_DIM), 1))
    hrow = lax.broadcasted_iota(jnp.int32, (W, W), 0) // HEAD_DIM
    hcol = lax.broadcasted_iota(jnp.int32, (W, W), 1) // HEAD_DIM
    head_ones = (hrow == hcol).astype(F32)

    kk = k * kk_ref[...]
    kk = kk * lax.rsqrt(_mm(kk * kk, head_ones, 3) + 1e-12)
    mm = functools.partial(_mm, passes=RW_P_ALG)

    for d in range(2):
        lw = jnp.tanh(zw[:, d * RW_LW:(d + 1) * RW_LW])
        la = za[:, d * RW_LA:(d + 1) * RW_LA]
        w_raw = w0_ref[d:d + 1, :] + _mm(lw, wup_ref[d], RW_P_LORA)
        softplus = jnp.maximum(-w_raw, 0.0) + jnp.log(1.0 + jnp.exp(-jnp.abs(w_raw)))
        logw = -jnp.exp(-softplus - 0.5)
        a = _sigmoid(a0_ref[d:d + 1, :] + _mm(la, aup_ref[d], RW_P_LORA))
        k_d = k * (1.0 + (a - 1.0) * ka_ref[...])
        b_d = kk * a
        if d == 0:
            tri_incl, tri_strict = col <= row, col < row
        else:
            tri_incl, tri_strict = col >= row, col > row
        cum = _mm(tri_incl.astype(F32), logw, 6)
        tot = jnp.sum(logw, axis=0, keepdims=True)
        e_neg = jnp.exp(-cum)
        e_end = jnp.exp(tot - cum)
        a_t = -kk * jnp.exp(cum - logw)
        r_t = r * jnp.exp(cum)
        b_t = b_d * e_neg
        k_t = k_d * e_neg
        b_p = b_d * e_end
        k_p = k_d * e_end
        p_c = jnp.exp(tot)
        for h in range(RW_H):
            sl = slice(h * HEAD_DIM, (h + 1) * HEAD_DIM)
            A, R, Bm, K, V = a_t[:, sl], r_t[:, sl], b_t[:, sl], k_t[:, sl], v[:, sl]
            BP, KP = b_p[:, sl], k_p[:, sl]
            l_ab = jnp.where(tri_strict, mm(A, Bm, nt=True), 0.0)
            l_ak = jnp.where(tri_strict, mm(A, K, nt=True), 0.0)
            m_rb = jnp.where(tri_incl, mm(R, Bm, nt=True), 0.0)
            m_rk = jnp.where(tri_incl, mm(R, K, nt=True), 0.0)
            t_inv = jnp.where(eye, 1.0, l_ab)
            pw = l_ab
            n = 2
            while n < C:
                pw = mm(pw, pw)
                t_inv = t_inv + mm(t_inv, pw)
                n *= 2
            x1 = mm(t_inv, A)
            x2 = mm(t_inv, mm(l_ak, V))
            hi = d * RW_H + h
            wr_ref[0, 0, hi] = R + mm(m_rb, x1)
            yin_ref[0, 0, hi] = mm(m_rb, x2) + mm(m_rk, V)
            bpt = BP.T
            phi_ref[0, 0, hi] = jnp.where(eye_j, p_c[:, sl], 0.0) + mm(bpt, x1)
            zc_ref[0, 0, hi] = mm(bpt, x2) + mm(KP.T, V)


def _rwkv_prep(zrw, w0, w_up, a0, a_up, k_k, k_a):
    B, T, _ = zrw.shape
    C = RW_CHUNK
    nch = T // C
    full = lambda a: pl.BlockSpec(a.shape, lambda b, c: (0,) * a.ndim)
    k_k = k_k.reshape(1, RW_W)
    k_a = k_a.reshape(1, RW_W)
    ospec = pl.BlockSpec((1, 1, 2 * RW_H, HEAD_DIM, HEAD_DIM), lambda b, c: (b, c, 0, 0, 0))
    oshape = jax.ShapeDtypeStruct((B, nch, 2 * RW_H, HEAD_DIM, HEAD_DIM), F32)
    return pl.pallas_call(
        _rwkv_prep_kernel,
        grid=(B, nch),
        in_specs=[pl.BlockSpec((1, C, RW_COLS), lambda b, c: (b, c, 0)),
                  full(w0), full(w_up), full(a0), full(a_up), full(k_k), full(k_a)],
        out_specs=[ospec] * 4,
        out_shape=[oshape] * 4,
        compiler_params=_cparams("parallel", "parallel"),
        name="rwkv_prep",
    )(zrw, w0, w_up, a0, a_up, k_k, k_a)


RW_P_STATE = 3


def _rwkv_state_kernel(phic, zcc, wrc, yinc, phil, zcl, wrl, yinl, yc_ref, yl_ref):
    d = pl.program_id(1)
    mm = functools.partial(_mm, passes=RW_P_STATE)

    def run(H, phi, zc, wr, yin, y_ref):
        nch = phi.shape[1]

        def body(s, H):
            c = jnp.where(d == 0, s, nch - 1 - s)
            t0 = pl.multiple_of(c * RW_CHUNK, RW_CHUNK)
            new = []
            for h in range(RW_H):
                y_ref[0, 0, pl.ds(t0, RW_CHUNK), h * HEAD_DIM:(h + 1) * HEAD_DIM] = \
                    mm(wr[0, c, h], H[h]) + yin[0, c, h]
                new.append(mm(phi[0, c, h], H[h]) + zc[0, c, h])
            return tuple(new)

        return lax.fori_loop(0, nch, body, H)

    H = tuple(jnp.zeros((HEAD_DIM, HEAD_DIM), F32) for _ in range(RW_H))
    H = run(H, phic, zcc, wrc, yinc, yc_ref)
    run(H, phil, zcl, wrl, yinl, yl_ref)


def _rwkv_state(ops_c, ops_l):
    B, nc = ops_c[0].shape[:2]
    nl = ops_l[0].shape[1]
    spec = lambda n: pl.BlockSpec((1, n, RW_H, HEAD_DIM, HEAD_DIM), lambda b, d: (b, 0, d, 0, 0))
    yspec = lambda n: pl.BlockSpec((1, 1, n * RW_CHUNK, RW_W), lambda b, d: (b, d, 0, 0))
    return pl.pallas_call(
        _rwkv_state_kernel,
        grid=(B, 2),
        in_specs=[spec(nc)] * 4 + [spec(nl)] * 4,
        out_specs=[yspec(nc), yspec(nl)],
        out_shape=[jax.ShapeDtypeStruct((B, 2, nc * RW_CHUNK, RW_W), F32),
                   jax.ShapeDtypeStruct((B, 2, nl * RW_CHUNK, RW_W), F32)],
        compiler_params=_cparams("parallel", "parallel"),
        name="rwkv_state",
    )(*ops_c, *ops_l)


def _rwkv_out_kernel(yf_ref, yr_ref, z_ref, gup_ref, rk_ref, lnx_ref, o_ref):
    W = RW_W
    y = yf_ref[0, 0] + yr_ref[0, 0]
    z = z_ref[0]
    r, k, v = z[:, 0:W], z[:, W:2 * W], z[:, 2 * W:3 * W]
    zg = z[:, RW_COLS - RW_LG:RW_COLS]
    hrow = lax.broadcasted_iota(jnp.int32, (W, W), 0) // HEAD_DIM
    hcol = lax.broadcasted_iota(jnp.int32, (W, W), 1) // HEAD_DIM
    head_ones = (hrow == hcol).astype(F32)
    inv = 1.0 / HEAD_DIM
    mu = _mm(y, head_ones, 3) * inv
    yc = y - mu
    var = _mm(yc * yc, head_ones, 3) * inv
    yn = yc * lax.rsqrt(var + GN_EPS) * lnx_ref[...]
    bonus = _mm(r * k * rk_ref[...], head_ones, 3)
    yn = yn + bonus * v
    gate = _mm(_sigmoid(zg), gup_ref[...], 3)
    o_ref[0] = yn * gate


def _rwkv_out(y2, zrw, g_up, r_k, lnx_g):
    B, _, T, W = y2.shape
    tm = 256
    return pl.pallas_call(
        _rwkv_out_kernel,
        grid=(B, T // tm),
        in_specs=[
            pl.BlockSpec((1, 1, tm, W), lambda b, t: (b, 0, t, 0)),
            pl.BlockSpec((1, 1, tm, W), lambda b, t: (b, 1, t, 0)),
            pl.BlockSpec((1, tm, RW_COLS), lambda b, t: (b, t, 0)),
            pl.BlockSpec((RW_LG, W), lambda b, t: (0, 0)),
            pl.BlockSpec((1, W), lambda b, t: (0, 0)),
            pl.BlockSpec((1, W), lambda b, t: (0, 0)),
        ],
        out_specs=pl.BlockSpec((1, tm, W), lambda b, t: (b, t, 0)),
        out_shape=jax.ShapeDtypeStruct((B, T, W), F32),
        compiler_params=_cparams("parallel", "parallel"),
        name="rwkv_out",
    )(y2, y2, zrw, g_up, r_k.reshape(1, W), lnx_g.reshape(1, W))


def _softmax_av(scores, values):
    m = functools.reduce(jnp.maximum, [jnp.max(s, axis=-1, keepdims=True) for s in scores])
    ps = [jnp.exp(s - m) for s in scores]
    l = functools.reduce(lambda a, b: a + b, [jnp.sum(p, axis=-1, keepdims=True) for p in ps])
    o = functools.reduce(lambda a, b: a + b,
                         [jnp.dot(p.astype(BF16), v, preferred_element_type=F32) for p, v in zip(ps, values)])
    return o / l


def _na_kernel(q_ref, k_ref, v_ref, kc_ref, vc_ref, bias_ref, o_ref, *, kr):
    r = pl.program_id(1)
    rows = pl.num_programs(1)
    start = pl.multiple_of(jnp.clip(r - NA_KR // 2, 0, rows - kr) * GRID_W, GRID_W)
    nwin = kr * GRID_W
    for h in range(NA_H):
        sl = slice(h * HEAD_DIM, (h + 1) * HEAD_DIM)
        q = q_ref[0, :, sl]
        kb = k_ref[0, pl.ds(start, nwin), sl]
        vb = v_ref[0, pl.ds(start, nwin), sl]
        s_win = _mm(q, kb, nt=True) + bias_ref[0, h]
        s_ctx = _mm(q, kc_ref[0, :, sl], nt=True)
        o_ref[0, :, sl] = _softmax_av([s_win, s_ctx], [vb, vc_ref[0, :, sl]])


def _na_bias_table(rpb, rows, kr):
    ci = np.arange(GRID_W)
    c0 = np.clip(ci - NA_KC // 2, 0, GRID_W - NA_KC)
    col_ok = (ci[None, :] >= c0[:, None]) & (ci[None, :] < c0[:, None] + NA_KC)
    dc = np.clip(ci[None, :] - ci[:, None] + (NA_KC - 1), 0, 2 * NA_KC - 2)
    n_off = NA_KR
    dr = np.arange(kr)[None, :] - np.arange(n_off)[:, None] + (NA_KR - 1)
    dr_ok = (dr >= 0) & (dr < 2 * NA_KR - 1)
    drc = np.clip(dr, 0, 2 * NA_KR - 2)
    bias = rpb[:, drc[:, None, :, None], dc[None, :, None, :]].astype(F32)
    ok = dr_ok[None, :, None, :, None] & col_ok[None, None, :, None, :]
    bias = jnp.where(ok, bias, NEG_INF)
    return jnp.transpose(bias, (1, 0, 2, 3, 4)).reshape(n_off, NA_H, GRID_W, kr * GRID_W)


def _na_attention(q, k, v, kc, vc, bias_tab):
    B, S, W = q.shape
    Cn = kc.shape[1]
    rows = S // GRID_W
    kr = min(NA_KR, rows)
    nwin = kr * GRID_W

    def bias_idx(b, r):
        return (r - jnp.clip(r - NA_KR // 2, 0, rows - kr), 0, 0, 0)

    return pl.pallas_call(
        functools.partial(_na_kernel, kr=kr),
        grid=(B, rows),
        in_specs=[
            pl.BlockSpec((1, GRID_W, W), lambda b, r: (b, r, 0)),
            pl.BlockSpec((1, S, W), lambda b, r: (b, 0, 0)),
            pl.BlockSpec((1, S, W), lambda b, r: (b, 0, 0)),
            pl.BlockSpec((1, Cn, W), lambda b, r: (b, 0, 0)),
            pl.BlockSpec((1, Cn, W), lambda b, r: (b, 0, 0)),
            pl.BlockSpec((1, NA_H, GRID_W, nwin), bias_idx),
        ],
        out_specs=pl.BlockSpec((1, GRID_W, W), lambda b, r: (b, r, 0)),
        out_shape=jax.ShapeDtypeStruct((B, S, W), F32),
        compiler_params=_cparams("parallel", "arbitrary"),
        name="na_attn",
    )(q, k, v, kc, vc, bias_tab)


def _ctx_attn_kernel(q_ref, k_ref, v_ref, o_ref):
    for h in range(NA_H):
        sl = slice(h * HEAD_DIM, (h + 1) * HEAD_DIM)
        s = _mm(q_ref[0, :, sl], k_ref[0, :, sl], nt=True)
        o_ref[0, :, sl] = _softmax_av([s], [v_ref[0, :, sl]])


def _ctx_attention(q, k, v):
    B, Cn, W = q.shape
    spec = pl.BlockSpec((1, Cn, W), lambda b: (b, 0, 0))
    return pl.pallas_call(
        _ctx_attn_kernel,
        grid=(B,),
        in_specs=[spec] * 3,
        out_specs=spec,
        out_shape=jax.ShapeDtypeStruct((B, Cn, W), F32),
        compiler_params=_cparams("parallel"),
        name="ctx_attn",
    )(q, k, v)


def _conv_kernel(z_ref, w_ref, o_ref):
    T = z_ref.shape[1]
    z = z_ref[0]
    zb, zc, zx = z[:, :SC_W], z[:, SC_W:2 * SC_W], z[:, 2 * SC_W:]
    u = zc * zx
    t = lax.broadcasted_iota(jnp.int32, u.shape, 0)
    prev = jnp.where(t == 0, 0.0, pltpu.roll(u, 1, 0))
    nxt = jnp.where(t == T - 1, 0.0, pltpu.roll(u, T - 1, 0))
    o_ref[0] = zb * (prev * w_ref[0:1, :] + u * w_ref[1:2, :] + nxt * w_ref[2:3, :])


def _short_conv(zsc, w):
    B, T, _ = zsc.shape
    return pl.pallas_call(
        _conv_kernel,
        grid=(B,),
        in_specs=[pl.BlockSpec((1, T, 3 * SC_W), lambda b: (b, 0, 0)),
                  pl.BlockSpec((3, SC_W), lambda b: (0, 0))],
        out_specs=pl.BlockSpec((1, T, SC_W), lambda b: (b, 0, 0)),
        out_shape=jax.ShapeDtypeStruct((B, T, SC_W), F32),
        compiler_params=_cparams("parallel"),
        name="short_conv",
    )(zsc, w)


def _proj_out_kernel(yrw_ref, yna_ref, ysc_ref, w_ref, x_ref, mod_ref, o_ref):
    a, b = RW_W, RW_W + NA_W
    y = (jnp.dot(yrw_ref[0].astype(BF16), w_ref[:a, :], preferred_element_type=F32)
         + jnp.dot(yna_ref[0].astype(BF16), w_ref[a:b, :], preferred_element_type=F32)
         + jnp.dot(ysc_ref[0].astype(BF16), w_ref[b:, :], preferred_element_type=F32))
    o_ref[0] = x_ref[0] + mod_ref[0, 2] * y


def _proj_out(yrw, yna, ysc, w_bf, x, mod):
    B, T, D = x.shape
    tm = 256
    row = lambda b, t: (b, t, 0)
    return pl.pallas_call(
        _proj_out_kernel,
        grid=(B, T // tm),
        in_specs=[
            pl.BlockSpec((1, tm, RW_W), row),
            pl.BlockSpec((1, tm, NA_W), row),
            pl.BlockSpec((1, tm, SC_W), row),
            pl.BlockSpec(w_bf.shape, lambda b, t: (0, 0)),
            pl.BlockSpec((1, tm, D), row),
            pl.BlockSpec((1, 6, 1, D), lambda b, t: (b, 0, 0, 0)),
        ],
        out_specs=pl.BlockSpec((1, tm, D), row),
        out_shape=jax.ShapeDtypeStruct((B, T, D), F32),
        compiler_params=_cparams("parallel", "parallel"),
        name="proj_out",
    )(yrw, yna, ysc, w_bf, x, mod)


PEER_P_ROUTE = 3


def _peer_query_kernel(x_ref, mod_ref, g_ref, wh_ref, wl_ref, sk_ref, h_ref, s_ref):
    h = _norm_mod(x_ref[0], g_ref[...], mod_ref[0, 4], mod_ref[0, 3])
    h_hi = h.astype(BF16)
    h_ref[0] = h_hi
    q = jnp.dot(h_hi, wh_ref[...], preferred_element_type=F32)
    if PEER_P_ROUTE == 3:
        h_lo = (h - h_hi.astype(F32)).astype(BF16)
        q = q + jnp.dot(h_lo, wh_ref[...], preferred_element_type=F32) \
            + jnp.dot(h_hi, wl_ref[...], preferred_element_type=F32)
    dq = sk_ref.shape[2]
    for hp in range(sk_ref.shape[0]):
        s_ref[0, hp] = _mm(sk_ref[hp], q[:, hp * dq:(hp + 1) * dq], PEER_P_ROUTE, nt=True)


def _peer_query(x, mod, g, qw_hi, qw_lo, sub_keys):
    B, T, D = x.shape
    tm = 256
    nq = qw_hi.shape[1]
    sk = sub_keys.reshape(2 * PEER_H, PEER_NKEYS, -1)
    return pl.pallas_call(
        _peer_query_kernel,
        grid=(B, T // tm),
        in_specs=[
            pl.BlockSpec((1, tm, D), lambda b, t: (b, t, 0)),
            pl.BlockSpec((1, 6, 1, D), lambda b, t: (b, 0, 0, 0)),
            pl.BlockSpec((1, D), lambda b, t: (0, 0)),
            pl.BlockSpec((D, nq), lambda b, t: (0, 0)),
            pl.BlockSpec((D, nq), lambda b, t: (0, 0)),
            pl.BlockSpec(sk.shape, lambda b, t: (0, 0, 0)),
        ],
        out_specs=[pl.BlockSpec((1, tm, D), lambda b, t: (b, t, 0)),
                   pl.BlockSpec((1, 2 * PEER_H, PEER_NKEYS, tm), lambda b, t: (b, 0, 0, t))],
        out_shape=[jax.ShapeDtypeStruct((B, T, D), BF16),
                   jax.ShapeDtypeStruct((B, 2 * PEER_H, PEER_NKEYS, T), F32)],
        compiler_params=_cparams("parallel", "parallel"),
        name="peer_query",
    )(x, mod, g.reshape(1, D), qw_hi, qw_lo, sk)


def _extract_top(s, n, on_pick):
    rows = lax.broadcasted_iota(jnp.int32, s.shape, 0).astype(F32)
    big = float(s.shape[0])
    for it in range(n):
        m = jnp.max(s, axis=0, keepdims=True)
        idx = jnp.min(jnp.where(s == m, rows, big), axis=0, keepdims=True)
        sel = rows == idx
        on_pick(it, m, sel)
        s = jnp.where(sel, -jnp.inf, s)


def _peer_route_kernel(s_ref, rank2_ref, cnt_ref, e1_ref, e2_ref, v1_scr, v2_scr, cand_scr):
    tn = s_ref.shape[3]
    for h in range(PEER_H):
        s1 = s_ref[0, 2 * h]
        s2 = s_ref[0, 2 * h + 1]
        ranks = []
        for s, scr in ((s1, v1_scr), (s2, v2_scr)):
            rank = [jnp.full(s.shape, float(PEER_NKEYS), F32)]

            def pick(it, m, sel, scr=scr, rank=rank):
                scr[it:it + 1, :] = m
                rank[0] = jnp.where(sel, float(it), rank[0])

            _extract_top(s, PEER_TOPK, pick)
            ranks.append(rank[0])
        rank1, rank2 = ranks
        cand_scr[...] = jnp.full(cand_scr.shape, -jnp.inf, F32)
        for ci, (a, b) in enumerate(_CAND):
            cand_scr[ci:ci + 1, :] = v1_scr[a:a + 1, :] + v2_scr[b:b + 1, :]
        picked = [jnp.zeros(cand_scr.shape, F32), jnp.zeros((1, tn), F32), None]

        def pick_c(it, m, sel, picked=picked):
            if it == 0:
                picked[2] = m
            picked[0] = jnp.where(sel, 1.0, picked[0])
            picked[1] = picked[1] + jnp.exp(m - picked[2])

        _extract_top(cand_scr[...], PEER_TOPK, pick_c)
        selmask, z = picked[0], picked[1]
        ca = lax.broadcasted_iota(jnp.int32, (PEER_TOPK, _NCAND_PAD), 0)
        cj = lax.broadcasted_iota(jnp.int32, (PEER_TOPK, _NCAND_PAD), 1)
        row_of = functools.reduce(jnp.logical_or, [(ca == a) & (cj == ci) for ci, (a, _) in enumerate(_CAND)])
        cnt_a = jnp.dot(row_of.astype(BF16), selmask.astype(BF16), preferred_element_type=F32)
        cnt = jnp.zeros(s1.shape, F32)
        for a in range(PEER_TOPK):
            cnt = jnp.where(rank1 == float(a), cnt_a[a:a + 1, :], cnt)
        rank2_ref[0, h] = rank2
        cnt_ref[0, h] = cnt
        e1_ref[0, h] = jnp.exp(s1 - v1_scr[0:1, :]) / z
        e2_ref[0, h] = jnp.exp(s2 - v2_scr[0:1, :])


def _peer_route(sT):
    B, _, NK, T = sT.shape
    tn = 256
    ospec = pl.BlockSpec((1, PEER_H, NK, tn), lambda b, t: (b, 0, 0, t))
    oshape = jax.ShapeDtypeStruct((B, PEER_H, NK, T), F32)
    return pl.pallas_call(
        _peer_route_kernel,
        grid=(B, T // tn),
        in_specs=[pl.BlockSpec((1, 2 * PEER_H, NK, tn), lambda b, t: (b, 0, 0, t))],
        out_specs=[ospec] * 4,
        out_shape=[oshape] * 4,
        scratch_shapes=[pltpu.VMEM((PEER_TOPK, tn), F32), pltpu.VMEM((PEER_TOPK, tn), F32),
                        pltpu.VMEM((_NCAND_PAD, tn), F32)],
        compiler_params=_cparams("parallel", "parallel"),
        name="peer_route",
    )(sT)


def _gelu_tanh(x):
    return 0.5 * x * (1.0 + jnp.tanh(np.sqrt(2.0 / np.pi).astype(np.float32) * (x + 0.044715 * (x * x * x))))


def _peer_expert_kernel(h_ref, rank2_ref, cnt_ref, e1_ref, e2_ref, u_ref, vt_ref, x_ref, mod_ref,
                        o_ref, acc_ref, *, rows_per_tile):
    e = pl.program_id(2)

    @pl.when(e == 0)
    def _():
        acc_ref[...] = jnp.zeros(acc_ref.shape, F32)

    act = _gelu_tanh(lax.dot_general(u_ref[...], h_ref[0], (((1,), (1,)), ((), ())),
                                     preferred_element_type=F32))
    pieces = []
    for ii in range(rows_per_tile):
        i = e * rows_per_tile + ii
        g = None
        for h in range(PEER_H):
            cnt_row = cnt_ref[0, h, pl.ds(i, 1), :]
            e1_row = e1_ref[0, h, pl.ds(i, 1), :]
            gh = jnp.where(rank2_ref[0, h] < cnt_row, e2_ref[0, h] * e1_row, 0.0)
            g = gh if g is None else g + gh
        pieces.append((g * act[ii * PEER_NKEYS:(ii + 1) * PEER_NKEYS, :]).astype(BF16))
    ga = jnp.concatenate(pieces, axis=0)
    acc_ref[...] += jnp.dot(vt_ref[...], ga, preferred_element_type=F32)

    @pl.when(e == pl.num_programs(2) - 1)
    def _():
        o_ref[0] = x_ref[0] + mod_ref[0, 5] * acc_ref[...].T


def _peer_expert(h2, route, u_bf, vt_bf, x, mod):
    B, T, D = x.shape
    tn = min(512, T)
    te = 512
    ne = u_bf.shape[0] // te
    rspec = pl.BlockSpec((1, PEER_H, PEER_NKEYS, tn), lambda b, t, e: (b, 0, 0, t))
    return pl.pallas_call(
        functools.partial(_peer_expert_kernel, rows_per_tile=te // PEER_NKEYS),
        grid=(B, T // tn, ne),
        in_specs=[
            pl.BlockSpec((1, tn, D), lambda b, t, e: (b, t, 0)),
            rspec, rspec, rspec, rspec,
            pl.BlockSpec((te, D), lambda b, t, e: (e, 0)),
            pl.BlockSpec((D, te), lambda b, t, e: (0, e)),
            pl.BlockSpec((1, tn, D), lambda b, t, e: (b, t, 0)),
            pl.BlockSpec((1, 6, 1, D), lambda b, t, e: (b, 0, 0, 0)),
        ],
        out_specs=pl.BlockSpec((1, tn, D), lambda b, t, e: (b, t, 0)),
        out_shape=jax.ShapeDtypeStruct((B, T, D), F32),
        scratch_shapes=[pltpu.VMEM((D, tn), F32)],
        compiler_params=_cparams("parallel", "parallel", "arbitrary"),
        name="peer_expert",
    )(h2, *route, u_bf, vt_bf, x, mod)


def _peer(x, mod, g, qw_hi, qw_lo, sub_keys, u_bf, vt_bf):
    h2, sT = _peer_query(x, mod, g, qw_hi, qw_lo, sub_keys)
    route = _peer_route(sT)
    return _peer_expert(h2, route, u_bf, vt_bf, x, mod)


def _final_norm_kernel(x_ref, g_ref, o_ref):
    x = x_ref[0]
    ms = jnp.mean(x * x, axis=-1, keepdims=True)
    o_ref[0] = x * lax.rsqrt(ms + NORM_EPS) * g_ref[...]


def _final_norm(x, g):
    B, T, D = x.shape
    tm = 512
    return pl.pallas_call(
        _final_norm_kernel,
        grid=(B, T // tm),
        in_specs=[pl.BlockSpec((1, tm, D), lambda b, t: (b, t, 0)),
                  pl.BlockSpec((1, D), lambda b, t: (0, 0))],
        out_specs=pl.BlockSpec((1, tm, D), lambda b, t: (b, t, 0)),
        out_shape=jax.ShapeDtypeStruct((B, T, D), F32),
        compiler_params=_cparams("parallel", "parallel"),
        name="final_norm",
    )(x, g.reshape(1, D))


def _split_bf16(w):
    hi = w.astype(BF16)
    return hi, (w - hi.astype(F32)).astype(BF16)


def kernel(x, c, ctx, c_ctx, ada_w, ada_b, norm1_g, norm2_g, w_in, rw_w0, rw_w_up, rw_a0, rw_a_up, rw_g_up,
           rw_k_k, rw_k_a, rw_r_k, rw_lnx_g, na_rpb, sc_conv_w, w_out, peer_q_w, peer_sub_keys, peer_u,
           peer_v, final_g):
    B, S, D = x.shape
    depth = ada_w.shape[0]
    rows = S // GRID_W
    kr = min(NA_KR, rows)
    n_mod = -(-(B + 1) // 8) * 8
    cc = jnp.concatenate([c, c_ctx[None, :], jnp.zeros((n_mod - B - 1, D), F32)], axis=0)
    mod = _ada(cc, ada_w, ada_b)
    xl, xc = x, ctx
    for i in range(depth):
        need_ctx = i < depth - 1
        mod_l = mod[i, :B].reshape(B, 6, 1, D)
        mod_c = jnp.broadcast_to(mod[i, B].reshape(1, 6, 1, D), (B, 6, 1, D))
        w_in_bf = w_in[i].astype(BF16)
        w_out_bf = w_out[i].astype(BF16)
        rw = (rw_w0[i], rw_w_up[i], rw_a0[i], rw_a_up[i], rw_k_k[i], rw_k_a[i])

        zrw_l, q_l, k_l, v_l, zsc_l = _proj_in(xl, mod_l, norm1_g[i], w_in_bf)
        zrw_c, q_c, k_c, v_c, zsc_c = _proj_in(xc, mod_c, norm1_g[i], w_in_bf)
        y2_c, y2_l = _rwkv_state(_rwkv_prep(zrw_c, *rw), _rwkv_prep(zrw_l, *rw))
        yrw_l = _rwkv_out(y2_l, zrw_l, rw_g_up[i], rw_r_k[i], rw_lnx_g[i])
        yna_l = _na_attention(q_l, k_l, v_l, k_c, v_c, _na_bias_table(na_rpb[i], rows, kr))
        ysc_l = _short_conv(zsc_l, sc_conv_w[i])
        xl = _proj_out(yrw_l, yna_l, ysc_l, w_out_bf, xl, mod_l)

        qw_hi, qw_lo = _split_bf16(peer_q_w[i])
        u_bf = peer_u[i].astype(BF16)
        vt_bf = peer_v[i].astype(BF16).T
        peer_args = (norm2_g[i], qw_hi, qw_lo, peer_sub_keys[i], u_bf, vt_bf)
        xl = _peer(xl, mod_l, *peer_args)
        if need_ctx:
            yrw_c = _rwkv_out(y2_c, zrw_c, rw_g_up[i], rw_r_k[i], rw_lnx_g[i])
            yna_c = _ctx_attention(q_c, k_c, v_c)
            ysc_c = _short_conv(zsc_c, sc_conv_w[i])
            xc = _proj_out(yrw_c, yna_c, ysc_c, w_out_bf, xc, mod_c)
            xc = _peer(xc, mod_c, *peer_args)
    return _final_norm(xl, final_g)
```

```python
import functools

import numpy as np
import jax
import jax.numpy as jnp
from jax import lax
from jax.experimental import pallas as pl
from jax.experimental.pallas import tpu as pltpu

F32 = jnp.float32
BF16 = jnp.bfloat16

HEAD_DIM = 64
GRID_W = 64
RW_H = 4
RW_W = RW_H * HEAD_DIM
RW_LW = 64
RW_LA = 64
RW_LG = 128
RW_COLS = 3 * RW_W + 2 * RW_LW + 2 * RW_LA + RW_LG
NA_H = 8
NA_W = NA_H * HEAD_DIM
NA_KR = 8
NA_KC = 16
SC_W = 256
PEER_H = 8
PEER_NKEYS = 128
PEER_TOPK = 16
NORM_EPS = 1e-6
GN_EPS = 64e-5
NEG_INF = -1e30
RW_CHUNK = 64

VMEM_LIMIT = 52 * 1024 * 1024

_CAND = [(a, b) for a in range(PEER_TOPK) for b in range(PEER_TOPK) if (a + 1) * (b + 1) <= PEER_TOPK]
_NCAND = len(_CAND)
_NCAND_PAD = -(-_NCAND // 8) * 8


def _cparams(*sem):
    return pltpu.CompilerParams(dimension_semantics=sem, vmem_limit_bytes=VMEM_LIMIT)


def _mm(a, b, passes=1, nt=False):
    dims = (((1,), (1,)), ((), ())) if nt else (((1,), (0,)), ((), ()))
    dg = functools.partial(lax.dot_general, dimension_numbers=dims, preferred_element_type=F32)
    if passes == 6:
        return dg(a.astype(F32), b.astype(F32), precision=lax.Precision.HIGHEST)
    a_hi = a.astype(BF16)
    b_hi = b.astype(BF16)
    out = dg(a_hi, b_hi)
    if passes == 3:
        a_lo = (a - a_hi.astype(F32)).astype(BF16)
        b_lo = (b - b_hi.astype(F32)).astype(BF16)
        out = out + dg(a_lo, b_hi) + dg(a_hi, b_lo)
    return out


def _sigmoid(x):
    return 1.0 / (1.0 + jnp.exp(-x))


def _norm_mod(x, g, scale, shift):
    ms = jnp.mean(x * x, axis=-1, keepdims=True)
    y = x * lax.rsqrt(ms + NORM_EPS) * g
    return y * (1.0 + scale) + shift


def _ada_kernel(c_ref, w_ref, b_ref, o_ref):
    cc = c_ref[...]
    o_ref[0] = _mm(cc * _sigmoid(cc), w_ref[0], 6) + b_ref[0]


def _ada(cc, ada_w, ada_b):
    L, D, D6 = ada_w.shape
    R = cc.shape[0]
    tn = 1536
    return pl.pallas_call(
        _ada_kernel,
        grid=(L, D6 // tn),
        in_specs=[
            pl.BlockSpec((R, D), lambda l, j: (0, 0)),
            pl.BlockSpec((1, D, tn), lambda l, j: (l, 0, j)),
            pl.BlockSpec((1, 1, tn), lambda l, j: (l, 0, j)),
        ],
        out_specs=pl.BlockSpec((1, R, tn), lambda l, j: (l, 0, j)),
        out_shape=jax.ShapeDtypeStruct((L, R, D6), F32),
        compiler_params=_cparams("parallel", "parallel"),
        name="ada",
    )(cc, ada_w, ada_b.reshape(L, 1, D6))


def _proj_in_kernel(x_ref, mod_ref, g_ref, w_ref, zrw_ref, q_ref, k_ref, v_ref, zsc_ref):
    h = _norm_mod(x_ref[0], g_ref[...], mod_ref[0, 1], mod_ref[0, 0])
    z = jnp.dot(h.astype(BF16), w_ref[...], preferred_element_type=F32)
    o = RW_COLS
    zrw_ref[0] = z[:, :o]
    q_ref[0] = (z[:, o:o + NA_W] * (HEAD_DIM ** -0.5)).astype(BF16)
    k_ref[0] = z[:, o + NA_W:o + 2 * NA_W].astype(BF16)
    v_ref[0] = z[:, o + 2 * NA_W:o + 3 * NA_W].astype(BF16)
    zsc_ref[0] = z[:, o + 3 * NA_W:]


def _proj_in(x, mod, g, w_bf):
    B, T, D = x.shape
    tm = 256
    n_out = w_bf.shape[1]
    row = lambda b, t: (b, t, 0)
    return pl.pallas_call(
        _proj_in_kernel,
        grid=(B, T // tm),
        in_specs=[
            pl.BlockSpec((1, tm, D), row),
            pl.BlockSpec((1, 6, 1, D), lambda b, t: (b, 0, 0, 0)),
            pl.BlockSpec((1, D), lambda b, t: (0, 0)),
            pl.BlockSpec((D, n_out), lambda b, t: (0, 0)),
        ],
        out_specs=[
            pl.BlockSpec((1, tm, RW_COLS), row),
            pl.BlockSpec((1, tm, NA_W), row),
            pl.BlockSpec((1, tm, NA_W), row),
            pl.BlockSpec((1, tm, NA_W), row),
            pl.BlockSpec((1, tm, 3 * SC_W), row),
        ],
        out_shape=[
            jax.ShapeDtypeStruct((B, T, RW_COLS), F32),
            jax.ShapeDtypeStruct((B, T, NA_W), BF16),
            jax.ShapeDtypeStruct((B, T, NA_W), BF16),
            jax.ShapeDtypeStruct((B, T, NA_W), BF16),
            jax.ShapeDtypeStruct((B, T, 3 * SC_W), F32),
        ],
        compiler_params=_cparams("parallel", "parallel"),
        name="proj_in",
    )(x, mod, g.reshape(1, D), w_bf)


RW_P_ALG = 1
RW_P_LORA = 3


def _rwkv_prep_kernel(z_ref, w0_ref, wup_ref, a0_ref, aup_ref, kk_ref, ka_ref,
                      phi_ref, zc_ref, wr_ref, yin_ref):
    C = z_ref.shape[1]
    W = RW_W
    z = z_ref[0]
    r = z[:, 0:W]
    k = z[:, W:2 * W]
    v = z[:, 2 * W:3 * W]
    zw = z[:, 3 * W:3 * W + 2 * RW_LW]
    za = z[:, 3 * W + 2 * RW_LW:3 * W + 2 * RW_LW + 2 * RW_LA]
    row = lax.broadcasted_iota(jnp.int32, (C, C), 0)
    col = lax.broadcasted_iota(jnp.int32, (C, C), 1)
    eye = row == col
    eye_j = (lax.broadcasted_iota(jnp.int32, (HEAD_DIM, HEAD_DIM), 0)
             == lax.broadcasted_iota(jnp.int32, (HEAD_DIM, HEAD_DIM), 1))
    hrow = lax.broadcasted_iota(jnp.int32, (W, W), 0) // HEAD_DIM
    hcol = lax.broadcasted_iota(jnp.int32, (W, W), 1) // HEAD_DIM
    head_ones = (hrow == hcol).astype(F32)

    kk = k * kk_ref[...]
    kk = kk * lax.rsqrt(_mm(kk * kk, head_ones, 3) + 1e-12)
    mm = functools.partial(_mm, passes=RW_P_ALG)

    for d in range(2):
        lw = jnp.tanh(zw[:, d * RW_LW:(d + 1) * RW_LW])
        la = za[:, d * RW_LA:(d + 1) * RW_LA]
        w_raw = w0_ref[d:d + 1, :] + _mm(lw, wup_ref[d], RW_P_LORA)
        softplus = jnp.maximum(-w_raw, 0.0) + jnp.log(1.0 + jnp.exp(-jnp.abs(w_raw)))
        logw = -jnp.exp(-softplus - 0.5)
        a = _sigmoid(a0_ref[d:d + 1, :] + _mm(la, aup_ref[d], RW_P_LORA))
        k_d = k * (1.0 + (a - 1.0) * ka_ref[...])
        b_d = kk * a
        if d == 0:
            tri_incl, tri_strict = col <= row, col < row
        else:
            tri_incl, tri_strict = col >= row, col > row
        cum = _mm(tri_incl.astype(F32), logw, 6)
        tot = jnp.sum(logw, axis=0, keepdims=True)
        e_neg = jnp.exp(-cum)
        e_end = jnp.exp(tot - cum)
        a_t = -kk * jnp.exp(cum - logw)
        r_t = r * jnp.exp(cum)
        b_t = b_d * e_neg
        k_t = k_d * e_neg
        b_p = b_d * e_end
        k_p = k_d * e_end
        p_c = jnp.exp(tot)
        for h in range(RW_H):
            sl = slice(h * HEAD_DIM, (h + 1) * HEAD_DIM)
            A, R, Bm, K, V = a_t[:, sl], r_t[:, sl], b_t[:, sl], k_t[:, sl], v[:, sl]
            BP, KP = b_p[:, sl], k_p[:, sl]
            l_ab = jnp.where(tri_strict, mm(A, Bm, nt=True), 0.0)
            l_ak = jnp.where(tri_strict, mm(A, K, nt=True), 0.0)
            m_rb = jnp.where(tri_incl, mm(R, Bm, nt=True), 0.0)
            m_rk = jnp.where(tri_incl, mm(R, K, nt=True), 0.0)
            t_inv = jnp.where(eye, 1.0, l_ab)
            pw = l_ab
            n = 2
            while n < C:
                pw = mm(pw, pw)
                t_inv = t_inv + mm(t_inv, pw)
                n *= 2
            x1 = mm(t_inv, A)
            x2 = mm(t_inv, mm(l_ak, V))
            hi = d * RW_H + h
            wr_ref[0, 0, hi] = R + mm(m_rb, x1)
            yin_ref[0, 0, hi] = mm(m_rb, x2) + mm(m_rk, V)
            bpt = BP.T
            phi_ref[0, 0, hi] = jnp.where(eye_j, p_c[:, sl], 0.0) + mm(bpt, x1)
            zc_ref[0, 0, hi] = mm(bpt, x2) + mm(KP.T, V)


def _rwkv_prep(zrw, w0, w_up, a0, a_up, k_k, k_a):
    B, T, _ = zrw.shape
    C = RW_CHUNK
    nch = T // C
    full = lambda a: pl.BlockSpec(a.shape, lambda b, c: (0,) * a.ndim)
    k_k = k_k.reshape(1, RW_W)
    k_a = k_a.reshape(1, RW_W)
    ospec = pl.BlockSpec((1, 1, 2 * RW_H, HEAD_DIM, HEAD_DIM), lambda b, c: (b, c, 0, 0, 0))
    oshape = jax.ShapeDtypeStruct((B, nch, 2 * RW_H, HEAD_DIM, HEAD_DIM), F32)
    return pl.pallas_call(
        _rwkv_prep_kernel,
        grid=(B, nch),
        in_specs=[pl.BlockSpec((1, C, RW_COLS), lambda b, c: (b, c, 0)),
                  full(w0), full(w_up), full(a0), full(a_up), full(k_k), full(k_a)],
        out_specs=[ospec] * 4,
        out_shape=[oshape] * 4,
        compiler_params=_cparams("parallel", "parallel"),
        name="rwkv_prep",
    )(zrw, w0, w_up, a0, a_up, k_k, k_a)


RW_P_STATE = 3


def _rwkv_state_kernel(phic, zcc, wrc, yinc, phil, zcl, wrl, yinl, yc_ref, yl_ref):
    d = pl.program_id(1)
    mm = functools.partial(_mm, passes=RW_P_STATE)

    def run(H, phi, zc, wr, yin, y_ref):
        nch = phi.shape[1]

        def body(s, H):
            c = jnp.where(d == 0, s, nch - 1 - s)
            t0 = pl.multiple_of(c * RW_CHUNK, RW_CHUNK)
            new = []
            for h in range(RW_H):
                y_ref[0, 0, pl.ds(t0, RW_CHUNK), h * HEAD_DIM:(h + 1) * HEAD_DIM] = \
                    mm(wr[0, c, h], H[h]) + yin[0, c, h]
                new.append(mm(phi[0, c, h], H[h]) + zc[0, c, h])
            return tuple(new)

        return lax.fori_loop(0, nch, body, H)

    H = tuple(jnp.zeros((HEAD_DIM, HEAD_DIM), F32) for _ in range(RW_H))
    H = run(H, phic, zcc, wrc, yinc, yc_ref)
    run(H, phil, zcl, wrl, yinl, yl_ref)


def _rwkv_state(ops_c, ops_l):
    B, nc = ops_c[0].shape[:2]
    nl = ops_l[0].shape[1]
    spec = lambda n: pl.BlockSpec((1, n, RW_H, HEAD_DIM, HEAD_DIM), lambda b, d: (b, 0, d, 0, 0))
    yspec = lambda n: pl.BlockSpec((1, 1, n * RW_CHUNK, RW_W), lambda b, d: (b, d, 0, 0))
    return pl.pallas_call(
        _rwkv_state_kernel,
        grid=(B, 2),
        in_specs=[spec(nc)] * 4 + [spec(nl)] * 4,
        out_specs=[yspec(nc), yspec(nl)],
        out_shape=[jax.ShapeDtypeStruct((B, 2, nc * RW_CHUNK, RW_W), F32),
                   jax.ShapeDtypeStruct((B, 2, nl * RW_CHUNK, RW_W), F32)],
        compiler_params=_cparams("parallel", "parallel"),
        name="rwkv_state",
    )(*ops_c, *ops_l)


def _rwkv_out_kernel(yf_ref, yr_ref, z_ref, gup_ref, rk_ref, lnx_ref, o_ref):
    W = RW_W
    y = yf_ref[0, 0] + yr_ref[0, 0]
    z = z_ref[0]
    r, k, v = z[:, 0:W], z[:, W:2 * W], z[:, 2 * W:3 * W]
    zg = z[:, RW_COLS - RW_LG:RW_COLS]
    hrow = lax.broadcasted_iota(jnp.int32, (W, W), 0) // HEAD_DIM
    hcol = lax.broadcasted_iota(jnp.int32, (W, W), 1) // HEAD_DIM
    head_ones = (hrow == hcol).astype(F32)
    inv = 1.0 / HEAD_DIM
    mu = _mm(y, head_ones, 3) * inv
    yc = y - mu
    var = _mm(yc * yc, head_ones, 3) * inv
    yn = yc * lax.rsqrt(var + GN_EPS) * lnx_ref[...]
    bonus = _mm(r * k * rk_ref[...], head_ones, 3)
    yn = yn + bonus * v
    gate = _mm(_sigmoid(zg), gup_ref[...], 3)
    o_ref[0] = yn * gate


def _rwkv_out(y2, zrw, g_up, r_k, lnx_g):
    B, _, T, W = y2.shape
    tm = 256
    return pl.pallas_call(
        _rwkv_out_kernel,
        grid=(B, T // tm),
        in_specs=[
            pl.BlockSpec((1, 1, tm, W), lambda b, t: (b, 0, t, 0)),
            pl.BlockSpec((1, 1, tm, W), lambda b, t: (b, 1, t, 0)),
            pl.BlockSpec((1, tm, RW_COLS), lambda b, t: (b, t, 0)),
            pl.BlockSpec((RW_LG, W), lambda b, t: (0, 0)),
            pl.BlockSpec((1, W), lambda b, t: (0, 0)),
            pl.BlockSpec((1, W), lambda b, t: (0, 0)),
        ],
        out_specs=pl.BlockSpec((1, tm, W), lambda b, t: (b, t, 0)),
        out_shape=jax.ShapeDtypeStruct((B, T, W), F32),
        compiler_params=_cparams("parallel", "parallel"),
        name="rwkv_out",
    )(y2, y2, zrw, g_up, r_k.reshape(1, W), lnx_g.reshape(1, W))


def _softmax_av(scores, values):
    m = functools.reduce(jnp.maximum, [jnp.max(s, axis=-1, keepdims=True) for s in scores])
    ps = [jnp.exp(s - m) for s in scores]
    l = functools.reduce(lambda a, b: a + b, [jnp.sum(p, axis=-1, keepdims=True) for p in ps])
    o = functools.reduce(lambda a, b: a + b,
                         [jnp.dot(p.astype(BF16), v, preferred_element_type=F32) for p, v in zip(ps, values)])
    return o / l


def _na_kernel(q_ref, k_ref, v_ref, kc_ref, vc_ref, bias_ref, o_ref, *, kr):
    r = pl.program_id(1)
    rows = pl.num_programs(1)
    start = pl.multiple_of(jnp.clip(r - NA_KR // 2, 0, rows - kr) * GRID_W, GRID_W)
    nwin = kr * GRID_W
    for h in range(NA_H):
        sl = slice(h * HEAD_DIM, (h + 1) * HEAD_DIM)
        q = q_ref[0, :, sl]
        kb = k_ref[0, pl.ds(start, nwin), sl]
        vb = v_ref[0, pl.ds(start, nwin), sl]
        s_win = _mm(q, kb, nt=True) + bias_ref[0, h]
        s_ctx = _mm(q, kc_ref[0, :, sl], nt=True)
        o_ref[0, :, sl] = _softmax_av([s_win, s_ctx], [vb, vc_ref[0, :, sl]])


def _na_bias_table(rpb, rows, kr):
    ci = np.arange(GRID_W)
    c0 = np.clip(ci - NA_KC // 2, 0, GRID_W - NA_KC)
    col_ok = (ci[None, :] >= c0[:, None]) & (ci[None, :] < c0[:, None] + NA_KC)
    dc = np.clip(ci[None, :] - ci[:, None] + (NA_KC - 1), 0, 2 * NA_KC - 2)
    n_off = NA_KR
    dr = np.arange(kr)[None, :] - np.arange(n_off)[:, None] + (NA_KR - 1)
    dr_ok = (dr >= 0) & (dr < 2 * NA_KR - 1)
    sel_r = (dr[:, :, None] == np.arange(2 * NA_KR - 1)).astype(np.float32)
    sel_c = (dc[:, :, None] == np.arange(2 * NA_KC - 1)).astype(np.float32)
    bias = jnp.einsum('hrc,okr,qwc->ohqkw', rpb.astype(F32), sel_r, sel_c, precision=lax.Precision.HIGHEST)
    ok = dr_ok[:, None, None, :, None] & col_ok[None, None, :, None, :]
    bias = jnp.where(ok, bias, NEG_INF)
    return bias.reshape(n_off, NA_H, GRID_W, kr * GRID_W)


def _na_attention(q, k, v, kc, vc, bias_tab):
    B, S, W = q.shape
    Cn = kc.shape[1]
    rows = S // GRID_W
    kr = min(NA_KR, rows)
    nwin = kr * GRID_W

    def bias_idx(b, r):
        return (r - jnp.clip(r - NA_KR // 2, 0, rows - kr), 0, 0, 0)

    return pl.pallas_call(
        functools.partial(_na_kernel, kr=kr),
        grid=(B, rows),
        in_specs=[
            pl.BlockSpec((1, GRID_W, W), lambda b, r: (b, r, 0)),
            pl.BlockSpec((1, S, W), lambda b, r: (b, 0, 0)),
            pl.BlockSpec((1, S, W), lambda b, r: (b, 0, 0)),
            pl.BlockSpec((1, Cn, W), lambda b, r: (b, 0, 0)),
            pl.BlockSpec((1, Cn, W), lambda b, r: (b, 0, 0)),
            pl.BlockSpec((1, NA_H, GRID_W, nwin), bias_idx),
        ],
        out_specs=pl.BlockSpec((1, GRID_W, W), lambda b, r: (b, r, 0)),
        out_shape=jax.ShapeDtypeStruct((B, S, W), F32),
        compiler_params=_cparams("parallel", "arbitrary"),
        name="na_attn",
    )(q, k, v, kc, vc, bias_tab)


def _ctx_attn_kernel(q_ref, k_ref, v_ref, o_ref):
    for h in range(NA_H):
        sl = slice(h * HEAD_DIM, (h + 1) * HEAD_DIM)
        s = _mm(q_ref[0, :, sl], k_ref[0, :, sl], nt=True)
        o_ref[0, :, sl] = _softmax_av([s], [v_ref[0, :, sl]])


def _ctx_attention(q, k, v):
    B, Cn, W = q.shape
    spec = pl.BlockSpec((1, Cn, W), lambda b: (b, 0, 0))
    return pl.pallas_call(
        _ctx_attn_kernel,
        grid=(B,),
        in_specs=[spec] * 3,
        out_specs=spec,
        out_shape=jax.ShapeDtypeStruct((B, Cn, W), F32),
        compiler_params=_cparams("parallel"),
        name="ctx_attn",
    )(q, k, v)


def _conv_kernel(z_ref, w_ref, o_ref):
    T = z_ref.shape[1]
    z = z_ref[0]
    zb, zc, zx = z[:, :SC_W], z[:, SC_W:2 * SC_W], z[:, 2 * SC_W:]
    u = zc * zx
    t = lax.broadcasted_iota(jnp.int32, u.shape, 0)
    prev = jnp.where(t == 0, 0.0, pltpu.roll(u, 1, 0))
    nxt = jnp.where(t == T - 1, 0.0, pltpu.roll(u, T - 1, 0))
    o_ref[0] = zb * (prev * w_ref[0:1, :] + u * w_ref[1:2, :] + nxt * w_ref[2:3, :])


def _short_conv(zsc, w):
    B, T, _ = zsc.shape
    return pl.pallas_call(
        _conv_kernel,
        grid=(B,),
        in_specs=[pl.BlockSpec((1, T, 3 * SC_W), lambda b: (b, 0, 0)),
                  pl.BlockSpec((3, SC_W), lambda b: (0, 0))],
        out_specs=pl.BlockSpec((1, T, SC_W), lambda b: (b, 0, 0)),
        out_shape=jax.ShapeDtypeStruct((B, T, SC_W), F32),
        compiler_params=_cparams("parallel"),
        name="short_conv",
    )(zsc, w)


def _proj_out_kernel(yrw_ref, yna_ref, ysc_ref, w_ref, x_ref, mod_ref, o_ref):
    a, b = RW_W, RW_W + NA_W
    y = (jnp.dot(yrw_ref[0].astype(BF16), w_ref[:a, :], preferred_element_type=F32)
         + jnp.dot(yna_ref[0].astype(BF16), w_ref[a:b, :], preferred_element_type=F32)
         + jnp.dot(ysc_ref[0].astype(BF16), w_ref[b:, :], preferred_element_type=F32))
    o_ref[0] = x_ref[0] + mod_ref[0, 2] * y


def _proj_out(yrw, yna, ysc, w_bf, x, mod):
    B, T, D = x.shape
    tm = 256
    row = lambda b, t: (b, t, 0)
    return pl.pallas_call(
        _proj_out_kernel,
        grid=(B, T // tm),
        in_specs=[
            pl.BlockSpec((1, tm, RW_W), row),
            pl.BlockSpec((1, tm, NA_W), row),
            pl.BlockSpec((1, tm, SC_W), row),
            pl.BlockSpec(w_bf.shape, lambda b, t: (0, 0)),
            pl.BlockSpec((1, tm, D), row),
            pl.BlockSpec((1, 6, 1, D), lambda b, t: (b, 0, 0, 0)),
        ],
        out_specs=pl.BlockSpec((1, tm, D), row),
        out_shape=jax.ShapeDtypeStruct((B, T, D), F32),
        compiler_params=_cparams("parallel", "parallel"),
        name="proj_out",
    )(yrw, yna, ysc, w_bf, x, mod)


PEER_P_ROUTE = 3


def _peer_query_kernel(x_ref, mod_ref, g_ref, wh_ref, wl_ref, sk_ref, ht_ref, s_ref):
    h = _norm_mod(x_ref[0], g_ref[...], mod_ref[0, 4], mod_ref[0, 3])
    h_hi = h.astype(BF16)
    ht_ref[0] = h.T.astype(BF16)
    q = jnp.dot(h_hi, wh_ref[...], preferred_element_type=F32)
    if PEER_P_ROUTE == 3:
        h_lo = (h - h_hi.astype(F32)).astype(BF16)
        q = q + jnp.dot(h_lo, wh_ref[...], preferred_element_type=F32) \
            + jnp.dot(h_hi, wl_ref[...], preferred_element_type=F32)
    dq = sk_ref.shape[2]
    for hp in range(sk_ref.shape[0]):
        s_ref[0, hp] = _mm(sk_ref[hp], q[:, hp * dq:(hp + 1) * dq], PEER_P_ROUTE, nt=True)


def _peer_query(x, mod, g, qw_hi, qw_lo, sub_keys):
    B, T, D = x.shape
    tm = 256
    nq = qw_hi.shape[1]
    sk = sub_keys.reshape(2 * PEER_H, PEER_NKEYS, -1)
    return pl.pallas_call(
        _peer_query_kernel,
        grid=(B, T // tm),
        in_specs=[
            pl.BlockSpec((1, tm, D), lambda b, t: (b, t, 0)),
            pl.BlockSpec((1, 6, 1, D), lambda b, t: (b, 0, 0, 0)),
            pl.BlockSpec((1, D), lambda b, t: (0, 0)),
            pl.BlockSpec((D, nq), lambda b, t: (0, 0)),
            pl.BlockSpec((D, nq), lambda b, t: (0, 0)),
            pl.BlockSpec(sk.shape, lambda b, t: (0, 0, 0)),
        ],
        out_specs=[pl.BlockSpec((1, D, tm), lambda b, t: (b, 0, t)),
                   pl.BlockSpec((1, 2 * PEER_H, PEER_NKEYS, tm), lambda b, t: (b, 0, 0, t))],
        out_shape=[jax.ShapeDtypeStruct((B, D, T), BF16),
                   jax.ShapeDtypeStruct((B, 2 * PEER_H, PEER_NKEYS, T), F32)],
        compiler_params=_cparams("parallel", "parallel"),
        name="peer_query",
    )(x, mod, g.reshape(1, D), qw_hi, qw_lo, sk)


def _extract_top(s, n, on_pick):
    rows = lax.broadcasted_iota(jnp.int32, s.shape, 0).astype(F32)
    big = float(s.shape[0])
    for it in range(n):
        m = jnp.max(s, axis=0, keepdims=True)
        idx = jnp.min(jnp.where(s == m, rows, big), axis=0, keepdims=True)
        sel = rows == idx
        on_pick(it, m, sel)
        s = jnp.where(sel, -jnp.inf, s)


def _dup_bf16(x):
    bits = lax.bitcast_convert_type(x.astype(BF16).astype(F32), jnp.uint32)
    return bits | (bits >> 16)


def _peer_route_kernel(s_ref, rank2_ref, cnt_ref, e1_ref, e2_ref, v1_scr, v2_scr, cand_scr):
    tn = s_ref.shape[3]
    for h in range(PEER_H):
        s1 = s_ref[0, 2 * h]
        s2 = s_ref[0, 2 * h + 1]
        ranks = []
        for s, scr in ((s1, v1_scr), (s2, v2_scr)):
            rank = [jnp.full(s.shape, float(PEER_NKEYS), F32)]

            def pick(it, m, sel, scr=scr, rank=rank):
                scr[it:it + 1, :] = m
                rank[0] = jnp.where(sel, float(it), rank[0])

            _extract_top(s, PEER_TOPK, pick)
            ranks.append(rank[0])
        rank1, rank2 = ranks
        cand_scr[...] = jnp.full(cand_scr.shape, -jnp.inf, F32)
        for ci, (a, b) in enumerate(_CAND):
            cand_scr[ci:ci + 1, :] = v1_scr[a:a + 1, :] + v2_scr[b:b + 1, :]
        picked = [jnp.zeros(cand_scr.shape, F32), jnp.zeros((1, tn), F32), None]

        def pick_c(it, m, sel, picked=picked):
            if it == 0:
                picked[2] = m
            picked[0] = jnp.where(sel, 1.0, picked[0])
            picked[1] = picked[1] + jnp.exp(m - picked[2])

        _extract_top(cand_scr[...], PEER_TOPK, pick_c)
        selmask, z = picked[0], picked[1]
        ca = lax.broadcasted_iota(jnp.int32, (PEER_TOPK, _NCAND_PAD), 0)
        cj = lax.broadcasted_iota(jnp.int32, (PEER_TOPK, _NCAND_PAD), 1)
        row_of = functools.reduce(jnp.logical_or, [(ca == a) & (cj == ci) for ci, (a, _) in enumerate(_CAND)])
        cnt_a = jnp.dot(row_of.astype(BF16), selmask.astype(BF16), preferred_element_type=F32)
        cnt = jnp.zeros(s1.shape, F32)
        for a in range(PEER_TOPK):
            cnt = jnp.where(rank1 == float(a), cnt_a[a:a + 1, :], cnt)
        rank2_ref[0, h] = rank2.astype(BF16)
        cnt_ref[0, h] = _dup_bf16(cnt)
        e1_ref[0, h] = _dup_bf16(jnp.exp(s1 - v1_scr[0:1, :]) / z)
        e2_ref[0, h] = jnp.exp(s2 - v2_scr[0:1, :]).astype(BF16)


def _peer_route(sT):
    B, _, NK, T = sT.shape
    tn = 256
    ospec = pl.BlockSpec((1, PEER_H, NK, tn), lambda b, t: (b, 0, 0, t))
    oshape = lambda dt: jax.ShapeDtypeStruct((B, PEER_H, NK, T), dt)
    return pl.pallas_call(
        _peer_route_kernel,
        grid=(B, T // tn),
        in_specs=[pl.BlockSpec((1, 2 * PEER_H, NK, tn), lambda b, t: (b, 0, 0, t))],
        out_specs=[ospec] * 4,
        out_shape=[oshape(BF16), oshape(jnp.uint32), oshape(jnp.uint32), oshape(BF16)],
        scratch_shapes=[pltpu.VMEM((PEER_TOPK, tn), F32), pltpu.VMEM((PEER_TOPK, tn), F32),
                        pltpu.VMEM((_NCAND_PAD, tn), F32)],
        compiler_params=_cparams("parallel", "parallel"),
        name="peer_route",
    )(sT)


_GELU_C = float(np.sqrt(2.0 / np.pi))
PEER_SUB = 256


def _gelu_tanh(x):
    hx = 0.5 * x
    return hx + hx * jnp.tanh(x * (_GELU_C + (_GELU_C * 0.044715) * (x * x)))


def _row_tile(words):
    return pltpu.bitcast(jnp.broadcast_to(words, (PEER_NKEYS // 2, words.shape[1])), BF16)


def _peer_expert_kernel(ht_ref, rank2_ref, cnt_ref, e1_ref, e2_ref, u_ref, vt_ref, x_ref, mod_ref,
                        o_ref, acc_ref, g_scr, *, te):
    e = pl.program_id(2)

    @pl.when(e == 0)
    def _():
        acc_ref[...] = jnp.zeros(acc_ref.shape, F32)

    ht = ht_ref[0]
    rows = te // PEER_NKEYS
    for ii in range(rows):
        i = e * rows + ii
        g = None
        for hd in range(PEER_H):
            cnt_row = _row_tile(cnt_ref[0, hd, pl.ds(i, 1), :])
            e1_row = _row_tile(e1_ref[0, hd, pl.ds(i, 1), :])
            gh = jnp.where(rank2_ref[0, hd] < cnt_row, e2_ref[0, hd] * e1_row, jnp.zeros((), BF16))
            g = gh if g is None else g + gh
        g_scr[ii * PEER_NKEYS:(ii + 1) * PEER_NKEYS, :] = g
    for sb in range(te // PEER_SUB):
        es = slice(sb * PEER_SUB, (sb + 1) * PEER_SUB)
        a = jnp.dot(u_ref[es, :], ht, preferred_element_type=F32)
        g_scr[es, :] = _gelu_tanh(a.astype(BF16)) * g_scr[es, :]
    acc_ref[...] += jnp.dot(vt_ref[...], g_scr[...], preferred_element_type=F32)

    @pl.when(e == pl.num_programs(2) - 1)
    def _():
        o_ref[0] = x_ref[0] + mod_ref[0, 5] * acc_ref[...].T


def _peer_expert(h2, route, u_bf, vt_bf, x, mod):
    B, T, D = x.shape
    tn = min(512, T)
    te = 1024
    ne = u_bf.shape[0] // te
    rspec = pl.BlockSpec((1, PEER_H, PEER_NKEYS, tn), lambda b, t, e: (b, 0, 0, t))
    return pl.pallas_call(
        functools.partial(_peer_expert_kernel, te=te),
        grid=(B, T // tn, ne),
        in_specs=[
            pl.BlockSpec((1, D, tn), lambda b, t, e: (b, 0, t)),
            rspec, rspec, rspec, rspec,
            pl.BlockSpec((te, D), lambda b, t, e: (e, 0)),
            pl.BlockSpec((D, te), lambda b, t, e: (0, e)),
            pl.BlockSpec((1, tn, D), lambda b, t, e: (b, t, 0)),
            pl.BlockSpec((1, 6, 1, D), lambda b, t, e: (b, 0, 0, 0)),
        ],
        out_specs=pl.BlockSpec((1, tn, D), lambda b, t, e: (b, t, 0)),
        out_shape=jax.ShapeDtypeStruct((B, T, D), F32),
        scratch_shapes=[pltpu.VMEM((D, tn), F32), pltpu.VMEM((te, tn), BF16)],
        compiler_params=_cparams("parallel", "parallel", "arbitrary"),
        name="peer_expert",
    )(h2, *route, u_bf, vt_bf, x, mod)


def _peer(x, mod, g, qw_hi, qw_lo, sub_keys, u_bf, vt_bf):
    h2, sT = _peer_query(x, mod, g, qw_hi, qw_lo, sub_keys)
    route = _peer_route(sT)
    return _peer_expert(h2, route, u_bf, vt_bf, x, mod)


def _final_norm_kernel(x_ref, g_ref, o_ref):
    x = x_ref[0]
    ms = jnp.mean(x * x, axis=-1, keepdims=True)
    o_ref[0] = x * lax.rsqrt(ms + NORM_EPS) * g_ref[...]


def _final_norm(x, g):
    B, T, D = x.shape
    tm = 512
    return pl.pallas_call(
        _final_norm_kernel,
        grid=(B, T // tm),
        in_specs=[pl.BlockSpec((1, tm, D), lambda b, t: (b, t, 0)),
                  pl.BlockSpec((1, D), lambda b, t: (0, 0))],
        out_specs=pl.BlockSpec((1, tm, D), lambda b, t: (b, t, 0)),
        out_shape=jax.ShapeDtypeStruct((B, T, D), F32),
        compiler_params=_cparams("parallel", "parallel"),
        name="final_norm",
    )(x, g.reshape(1, D))


def _split_bf16(w):
    hi = w.astype(BF16)
    return hi, (w - hi.astype(F32)).astype(BF16)


def kernel(x, c, ctx, c_ctx, ada_w, ada_b, norm1_g, norm2_g, w_in, rw_w0, rw_w_up, rw_a0, rw_a_up, rw_g_up,
           rw_k_k, rw_k_a, rw_r_k, rw_lnx_g, na_rpb, sc_conv_w, w_out, peer_q_w, peer_sub_keys, peer_u,
           peer_v, final_g):
    B, S, D = x.shape
    depth = ada_w.shape[0]
    rows = S // GRID_W
    kr = min(NA_KR, rows)
    n_mod = -(-(B + 1) // 8) * 8
    cc = jnp.concatenate([c, c_ctx[None, :], jnp.zeros((n_mod - B - 1, D), F32)], axis=0)
    mod = _ada(cc, ada_w, ada_b)
    xl, xc = x, ctx
    for i in range(depth):
        need_ctx = i < depth - 1
        mod_l = mod[i, :B].reshape(B, 6, 1, D)
        mod_c = jnp.broadcast_to(mod[i, B].reshape(1, 6, 1, D), (B, 6, 1, D))
        w_in_bf = w_in[i].astype(BF16)
        w_out_bf = w_out[i].astype(BF16)
        rw = (rw_w0[i], rw_w_up[i], rw_a0[i], rw_a_up[i], rw_k_k[i], rw_k_a[i])

        zrw_l, q_l, k_l, v_l, zsc_l = _proj_in(xl, mod_l, norm1_g[i], w_in_bf)
        zrw_c, q_c, k_c, v_c, zsc_c = _proj_in(xc, mod_c, norm1_g[i], w_in_bf)
        y2_c, y2_l = _rwkv_state(_rwkv_prep(zrw_c, *rw), _rwkv_prep(zrw_l, *rw))
        yrw_l = _rwkv_out(y2_l, zrw_l, rw_g_up[i], rw_r_k[i], rw_lnx_g[i])
        yna_l = _na_attention(q_l, k_l, v_l, k_c, v_c, _na_bias_table(na_rpb[i], rows, kr))
        ysc_l = _short_conv(zsc_l, sc_conv_w[i])
        xl = _proj_out(yrw_l, yna_l, ysc_l, w_out_bf, xl, mod_l)

        qw_hi, qw_lo = _split_bf16(peer_q_w[i])
        u_bf = peer_u[i].astype(BF16)
        vt_bf = peer_v[i].astype(BF16).T
        peer_args = (norm2_g[i], qw_hi, qw_lo, peer_sub_keys[i], u_bf, vt_bf)
        xl = _peer(xl, mod_l, *peer_args)
        if need_ctx:
            yrw_c = _rwkv_out(y2_c, zrw_c, rw_g_up[i], rw_r_k[i], rw_lnx_g[i])
            yna_c = _ctx_attention(q_c, k_c, v_c)
            ysc_c = _short_conv(zsc_c, sc_conv_w[i])
            xc = _proj_out(yrw_c, yna_c, ysc_c, w_out_bf, xc, mod_c)
            xc = _peer(xc, mod_c, *peer_args)
    return _final_norm(xl, final_g)
```

```python
import functools

import numpy as np
import jax
import jax.numpy as jnp
from jax import lax
from jax.experimental import pallas as pl
from jax.experimental.pallas import tpu as pltpu

F32 = jnp.float32
BF16 = jnp.bfloat16

HEAD_DIM = 64
GRID_W = 64
RW_H = 4
RW_W = RW_H * HEAD_DIM
RW_LW = 64
RW_LA = 64
RW_LG = 128
RW_COLS = 3 * RW_W + 2 * RW_LW + 2 * RW_LA + RW_LG
NA_H = 8
NA_W = NA_H * HEAD_DIM
NA_KR = 8
NA_KC = 16
SC_W = 256
PEER_H = 8
PEER_NKEYS = 128
PEER_TOPK = 16
NORM_EPS = 1e-6
GN_EPS = 64e-5
NEG_INF = -1e30
RW_CHUNK = 64

VMEM_LIMIT = 52 * 1024 * 1024

_CAND = [(a, b) for a in range(PEER_TOPK) for b in range(PEER_TOPK) if (a + 1) * (b + 1) <= PEER_TOPK]
_NCAND = len(_CAND)
_NCAND_PAD = -(-_NCAND // 8) * 8


def _cparams(*sem):
    return pltpu.CompilerParams(dimension_semantics=sem, vmem_limit_bytes=VMEM_LIMIT)


def _mm(a, b, passes=1, nt=False):
    dims = (((1,), (1,)), ((), ())) if nt else (((1,), (0,)), ((), ()))
    dg = functools.partial(lax.dot_general, dimension_numbers=dims, preferred_element_type=F32)
    if passes == 6:
        return dg(a.astype(F32), b.astype(F32), precision=lax.Precision.HIGHEST)
    a_hi = a.astype(BF16)
    b_hi = b.astype(BF16)
    out = dg(a_hi, b_hi)
    if passes == 3:
        a_lo = (a - a_hi.astype(F32)).astype(BF16)
        b_lo = (b - b_hi.astype(F32)).astype(BF16)
        out = out + dg(a_lo, b_hi) + dg(a_hi, b_lo)
    return out


def _sigmoid(x):
    return 1.0 / (1.0 + jnp.exp(-x))


def _norm_mod(x, g, scale, shift):
    ms = jnp.mean(x * x, axis=-1, keepdims=True)
    y = x * lax.rsqrt(ms + NORM_EPS) * g
    return y * (1.0 + scale) + shift


def _ada_kernel(c_ref, w_ref, b_ref, o_ref):
    cc = c_ref[...]
    o_ref[0] = _mm(cc * _sigmoid(cc), w_ref[0], 6) + b_ref[0]


def _ada(cc, ada_w, ada_b):
    L, D, D6 = ada_w.shape
    R = cc.shape[0]
    tn = 1536
    return pl.pallas_call(
        _ada_kernel,
        grid=(L, D6 // tn),
        in_specs=[
            pl.BlockSpec((R, D), lambda l, j: (0, 0)),
            pl.BlockSpec((1, D, tn), lambda l, j: (l, 0, j)),
            pl.BlockSpec((1, 1, tn), lambda l, j: (l, 0, j)),
        ],
        out_specs=pl.BlockSpec((1, R, tn), lambda l, j: (l, 0, j)),
        out_shape=jax.ShapeDtypeStruct((L, R, D6), F32),
        compiler_params=_cparams("parallel", "parallel"),
        name="ada",
    )(cc, ada_w, ada_b.reshape(L, 1, D6))


def _proj_in_kernel(x_ref, mod_ref, g_ref, w_ref, zrw_ref, q_ref, k_ref, v_ref, zsc_ref):
    h = _norm_mod(x_ref[0], g_ref[...], mod_ref[0, 1], mod_ref[0, 0])
    z = jnp.dot(h.astype(BF16), w_ref[...], preferred_element_type=F32)
    o = RW_COLS
    zrw_ref[0] = z[:, :o]
    q_ref[0] = (z[:, o:o + NA_W] * (HEAD_DIM ** -0.5)).astype(BF16)
    k_ref[0] = z[:, o + NA_W:o + 2 * NA_W].astype(BF16)
    v_ref[0] = z[:, o + 2 * NA_W:o + 3 * NA_W].astype(BF16)
    zsc_ref[0] = z[:, o + 3 * NA_W:]


def _proj_in(x, mod, g, w_bf):
    B, T, D = x.shape
    tm = 256
    n_out = w_bf.shape[1]
    row = lambda b, t: (b, t, 0)
    return pl.pallas_call(
        _proj_in_kernel,
        grid=(B, T // tm),
        in_specs=[
            pl.BlockSpec((1, tm, D), row),
            pl.BlockSpec((1, 6, 1, D), lambda b, t: (b, 0, 0, 0)),
            pl.BlockSpec((1, D), lambda b, t: (0, 0)),
            pl.BlockSpec((D, n_out), lambda b, t: (0, 0)),
        ],
        out_specs=[
            pl.BlockSpec((1, tm, RW_COLS), row),
            pl.BlockSpec((1, tm, NA_W), row),
            pl.BlockSpec((1, tm, NA_W), row),
            pl.BlockSpec((1, tm, NA_W), row),
            pl.BlockSpec((1, tm, 3 * SC_W), row),
        ],
        out_shape=[
            jax.ShapeDtypeStruct((B, T, RW_COLS), F32),
            jax.ShapeDtypeStruct((B, T, NA_W), BF16),
            jax.ShapeDtypeStruct((B, T, NA_W), BF16),
            jax.ShapeDtypeStruct((B, T, NA_W), BF16),
            jax.ShapeDtypeStruct((B, T, 3 * SC_W), F32),
        ],
        compiler_params=_cparams("parallel", "parallel"),
        name="proj_in",
    )(x, mod, g.reshape(1, D), w_bf)


RW_CHUNKS_PER_STEP = 2
RW_P_ALG = 1
RW_P_LORA = 3


def _rwkv_prep_kernel(z_ref, w0_ref, wup_ref, a0_ref, aup_ref, kk_ref, ka_ref,
                      phi_ref, zc_ref, wr_ref, yin_ref):
    C = RW_CHUNK
    W = RW_W
    assert C == HEAD_DIM
    N = RW_H * C
    row = lax.broadcasted_iota(jnp.int32, (C, C), 0)
    col = lax.broadcasted_iota(jnp.int32, (C, C), 1)
    brow = lax.broadcasted_iota(jnp.int32, (N, W), 0)
    bcol = lax.broadcasted_iota(jnp.int32, (N, W), 1)
    bd = (brow // C) == (bcol // HEAD_DIM)
    eye_j = lax.broadcasted_iota(jnp.int32, (HEAD_DIM, W), 0) == \
        lax.broadcasted_iota(jnp.int32, (HEAD_DIM, W), 1) % HEAD_DIM
    head_ones = bd.astype(F32)
    mm = functools.partial(_mm, passes=RW_P_ALG)
    mm_tn = lambda a, b: lax.dot_general(a.astype(BF16), b.astype(BF16), (((0,), (0,)), ((), ())),
                                         preferred_element_type=F32)
    collapse = lambda m: functools.reduce(lambda a, b: a + b, [m[h * C:(h + 1) * C] for h in range(RW_H)])

    def direction(ci, d):
        z = z_ref[0, ci * C:(ci + 1) * C, :]
        r = z[:, 0:W]
        k = z[:, W:2 * W]
        v = z[:, 2 * W:3 * W]
        zw = z[:, 3 * W:3 * W + 2 * RW_LW]
        za = z[:, 3 * W + 2 * RW_LW:3 * W + 2 * RW_LW + 2 * RW_LA]
        kk = k * kk_ref[...]
        kk = kk * lax.rsqrt(_mm(kk * kk, head_ones, 3) + 1e-12)
        lw = jnp.tanh(zw[:, d * RW_LW:(d + 1) * RW_LW])
        la = za[:, d * RW_LA:(d + 1) * RW_LA]
        w_raw = w0_ref[d:d + 1, :] + _mm(lw, wup_ref[d], RW_P_LORA)
        softplus = jnp.maximum(-w_raw, 0.0) + jnp.log(1.0 + jnp.exp(-jnp.abs(w_raw)))
        logw = -jnp.exp(-softplus - 0.5)
        a = _sigmoid(a0_ref[d:d + 1, :] + _mm(la, aup_ref[d], RW_P_LORA))
        k_d = k * (1.0 + (a - 1.0) * ka_ref[...])
        b_d = kk * a
        if d == 0:
            tri_incl, tri_strict = col <= row, col < row
        else:
            tri_incl, tri_strict = col >= row, col > row
        cum = _mm(tri_incl.astype(F32), logw, 6)
        tot = jnp.sum(logw, axis=0, keepdims=True)
        e_neg = jnp.exp(-cum)
        e_end = jnp.exp(tot - cum)
        a_t = -kk * jnp.exp(cum - logw)
        r_t = r * jnp.exp(cum)
        b_t = b_d * e_neg
        k_t = k_d * e_neg
        b_p = b_d * e_end
        k_p = k_d * e_end
        p_c = jnp.exp(tot)
        bdx = lambda x: jnp.where(bd, jnp.concatenate([x] * RW_H, axis=0), 0.0)
        a_bd, r_bd, v_bd = bdx(a_t), bdx(r_t), bdx(v)
        yield
        p4 = mm(jnp.concatenate([a_bd, r_bd], axis=0),
                jnp.concatenate([bdx(b_t), bdx(k_t)], axis=0), nt=True)
        yield
        tri_strict_bd = bd & ((bcol < brow) if d == 0 else (bcol > brow))
        tri_incl_bd = bd & ((bcol <= brow) if d == 0 else (bcol >= brow))
        l_ab = jnp.where(tri_strict_bd, p4[:N, :N], 0.0)
        l_ak = jnp.where(tri_strict_bd, p4[:N, N:], 0.0)
        m_rb = jnp.where(tri_incl_bd, p4[N:, :N], 0.0)
        m_rk = jnp.where(tri_incl_bd, p4[N:, N:], 0.0)
        t_inv = jnp.where(brow == bcol, 1.0, l_ab)
        pw = l_ab
        n = 2
        lakv = mm(l_ak, v_bd)
        mrkv = mm(m_rk, v_bd)
        kpv = mm_tn(bdx(k_p), v_bd)
        while n < C:
            pw = mm(pw, pw)
            yield
            t_inv = t_inv + mm(t_inv, pw)
            n *= 2
        yield
        x = mm(t_inv, jnp.concatenate([a_bd, lakv], axis=1))
        yield
        wy = mm(m_rb, x)
        pz = mm_tn(bdx(b_p), x)
        wr_ref[0, ci, d] = r_t + collapse(wy[:, :W])
        yin_ref[0, ci, d] = collapse(wy[:, W:] + mrkv)
        phi_ref[0, ci, d] = jnp.where(eye_j, p_c, 0.0) + collapse(pz[:, :W])
        zc_ref[0, ci, d] = collapse(pz[:, W:] + kpv)

    live = [direction(ci, d) for ci in range(z_ref.shape[1] // C) for d in range(2)]
    done = object()
    while live:
        live = [g for g in live if next(g, done) is not done]


def _rwkv_prep(zrw, w0, w_up, a0, a_up, k_k, k_a):
    B, T, _ = zrw.shape
    C = RW_CHUNK
    nch = T // C
    full = lambda a: pl.BlockSpec(a.shape, lambda b, c: (0,) * a.ndim)
    k_k = k_k.reshape(1, RW_W)
    k_a = k_a.reshape(1, RW_W)
    cps = RW_CHUNKS_PER_STEP
    ospec = pl.BlockSpec((1, cps, 2, HEAD_DIM, RW_W), lambda b, c: (b, c, 0, 0, 0))
    oshape = jax.ShapeDtypeStruct((B, nch, 2, HEAD_DIM, RW_W), F32)
    return pl.pallas_call(
        _rwkv_prep_kernel,
        grid=(B, nch // cps),
        in_specs=[pl.BlockSpec((1, cps * C, RW_COLS), lambda b, c: (b, c, 0)),
                  full(w0), full(w_up), full(a0), full(a_up), full(k_k), full(k_a)],
        out_specs=[ospec] * 4,
        out_shape=[oshape] * 4,
        compiler_params=_cparams("parallel", "parallel"),
        name="rwkv_prep",
    )(zrw, w0, w_up, a0, a_up, k_k, k_a)


RW_P_STATE = 3


def _rwkv_state_kernel(phic, zcc, wrc, yinc, phil, zcl, wrl, yinl, yc_ref, yl_ref):
    d = pl.program_id(1)
    mm = functools.partial(_mm, passes=RW_P_STATE)

    N = RW_H * HEAD_DIM
    bd = (lax.broadcasted_iota(jnp.int32, (N, RW_W), 0) // HEAD_DIM) == \
        (lax.broadcasted_iota(jnp.int32, (N, RW_W), 1) // HEAD_DIM)

    def run(H, phi, zc, wr, yin, y_ref):
        nch = phi.shape[1]

        def body(s, H):
            c = jnp.where(d == 0, s, nch - 1 - s)
            t0 = pl.multiple_of(c * RW_CHUNK, RW_CHUNK)
            h_bd = jnp.where(bd, jnp.concatenate([H] * RW_H, axis=0), 0.0)
            ops = jnp.concatenate([wr[0, c, 0], phi[0, c, 0]], axis=0)
            out = mm(ops, h_bd) + jnp.concatenate([yin[0, c, 0], zc[0, c, 0]], axis=0)
            y_ref[0, 0, pl.ds(t0, RW_CHUNK), :] = out[:RW_CHUNK]
            return out[RW_CHUNK:]

        return lax.fori_loop(0, nch, body, H)

    H = jnp.zeros((HEAD_DIM, RW_W), F32)
    H = run(H, phic, zcc, wrc, yinc, yc_ref)
    run(H, phil, zcl, wrl, yinl, yl_ref)


def _rwkv_state(ops_c, ops_l):
    B, nc = ops_c[0].shape[:2]
    nl = ops_l[0].shape[1]
    spec = lambda n: pl.BlockSpec((1, n, 1, HEAD_DIM, RW_W), lambda b, d: (b, 0, d, 0, 0))
    yspec = lambda n: pl.BlockSpec((1, 1, n * RW_CHUNK, RW_W), lambda b, d: (b, d, 0, 0))
    return pl.pallas_call(
        _rwkv_state_kernel,
        grid=(B, 2),
        in_specs=[spec(nc)] * 4 + [spec(nl)] * 4,
        out_specs=[yspec(nc), yspec(nl)],
        out_shape=[jax.ShapeDtypeStruct((B, 2, nc * RW_CHUNK, RW_W), F32),
                   jax.ShapeDtypeStruct((B, 2, nl * RW_CHUNK, RW_W), F32)],
        compiler_params=_cparams("parallel", "parallel"),
        name="rwkv_state",
    )(*ops_c, *ops_l)


def _rwkv_out_kernel(yf_ref, yr_ref, z_ref, gup_ref, rk_ref, lnx_ref, o_ref):
    W = RW_W
    y = yf_ref[0, 0] + yr_ref[0, 0]
    z = z_ref[0]
    r, k, v = z[:, 0:W], z[:, W:2 * W], z[:, 2 * W:3 * W]
    zg = z[:, RW_COLS - RW_LG:RW_COLS]
    hrow = lax.broadcasted_iota(jnp.int32, (W, W), 0) // HEAD_DIM
    hcol = lax.broadcasted_iota(jnp.int32, (W, W), 1) // HEAD_DIM
    head_ones = (hrow == hcol).astype(F32)
    inv = 1.0 / HEAD_DIM
    mu = _mm(y, head_ones, 3) * inv
    yc = y - mu
    var = _mm(yc * yc, head_ones, 3) * inv
    yn = yc * lax.rsqrt(var + GN_EPS) * lnx_ref[...]
    bonus = _mm(r * k * rk_ref[...], head_ones, 3)
    yn = yn + bonus * v
    gate = _mm(_sigmoid(zg), gup_ref[...], 3)
    o_ref[0] = yn * gate


def _rwkv_out(y2, zrw, g_up, r_k, lnx_g):
    B, _, T, W = y2.shape
    tm = 256
    return pl.pallas_call(
        _rwkv_out_kernel,
        grid=(B, T // tm),
        in_specs=[
            pl.BlockSpec((1, 1, tm, W), lambda b, t: (b, 0, t, 0)),
            pl.BlockSpec((1, 1, tm, W), lambda b, t: (b, 1, t, 0)),
            pl.BlockSpec((1, tm, RW_COLS), lambda b, t: (b, t, 0)),
            pl.BlockSpec((RW_LG, W), lambda b, t: (0, 0)),
            pl.BlockSpec((1, W), lambda b, t: (0, 0)),
            pl.BlockSpec((1, W), lambda b, t: (0, 0)),
        ],
        out_specs=pl.BlockSpec((1, tm, W), lambda b, t: (b, t, 0)),
        out_shape=jax.ShapeDtypeStruct((B, T, W), F32),
        compiler_params=_cparams("parallel", "parallel"),
        name="rwkv_out",
    )(y2, y2, zrw, g_up, r_k.reshape(1, W), lnx_g.reshape(1, W))


def _softmax_av(scores, values):
    m = functools.reduce(jnp.maximum, [jnp.max(s, axis=-1, keepdims=True) for s in scores])
    ps = [jnp.exp(s - m) for s in scores]
    l = functools.reduce(lambda a, b: a + b, [jnp.sum(p, axis=-1, keepdims=True) for p in ps])
    o = functools.reduce(lambda a, b: a + b,
                         [jnp.dot(p.astype(BF16), v, preferred_element_type=F32) for p, v in zip(ps, values)])
    return o / l


def _na_kernel(q_ref, k_ref, v_ref, kc_ref, vc_ref, bias_ref, o_ref, *, kr):
    r = pl.program_id(1)
    rows = pl.num_programs(1)
    start = pl.multiple_of(jnp.clip(r - NA_KR // 2, 0, rows - kr) * GRID_W, GRID_W)
    nwin = kr * GRID_W
    for h in range(NA_H):
        sl = slice(h * HEAD_DIM, (h + 1) * HEAD_DIM)
        q = q_ref[0, :, sl]
        kb = k_ref[0, pl.ds(start, nwin), sl]
        vb = v_ref[0, pl.ds(start, nwin), sl]
        s_win = _mm(q, kb, nt=True) + bias_ref[0, h]
        s_ctx = _mm(q, kc_ref[0, :, sl], nt=True)
        o_ref[0, :, sl] = _softmax_av([s_win, s_ctx], [vb, vc_ref[0, :, sl]])


def _na_bias_table(rpb, rows, kr):
    ci = np.arange(GRID_W)
    c0 = np.clip(ci - NA_KC // 2, 0, GRID_W - NA_KC)
    col_ok = (ci[None, :] >= c0[:, None]) & (ci[None, :] < c0[:, None] + NA_KC)
    dc = np.clip(ci[None, :] - ci[:, None] + (NA_KC - 1), 0, 2 * NA_KC - 2)
    n_off = NA_KR
    dr = np.arange(kr)[None, :] - np.arange(n_off)[:, None] + (NA_KR - 1)
    dr_ok = (dr >= 0) & (dr < 2 * NA_KR - 1)
    sel_r = (dr[:, :, None] == np.arange(2 * NA_KR - 1)).astype(np.float32)
    sel_c = (dc[:, :, None] == np.arange(2 * NA_KC - 1)).astype(np.float32)
    bias = jnp.einsum('hrc,okr,qwc->ohqkw', rpb.astype(F32), sel_r, sel_c, precision=lax.Precision.HIGHEST)
    ok = dr_ok[:, None, None, :, None] & col_ok[None, None, :, None, :]
    bias = jnp.where(ok, bias, NEG_INF)
    return bias.reshape(n_off, NA_H, GRID_W, kr * GRID_W)


def _na_attention(q, k, v, kc, vc, bias_tab):
    B, S, W = q.shape
    Cn = kc.shape[1]
    rows = S // GRID_W
    kr = min(NA_KR, rows)
    nwin = kr * GRID_W

    def bias_idx(b, r):
        return (r - jnp.clip(r - NA_KR // 2, 0, rows - kr), 0, 0, 0)

    return pl.pallas_call(
        functools.partial(_na_kernel, kr=kr),
        grid=(B, rows),
        in_specs=[
            pl.BlockSpec((1, GRID_W, W), lambda b, r: (b, r, 0)),
            pl.BlockSpec((1, S, W), lambda b, r: (b, 0, 0)),
            pl.BlockSpec((1, S, W), lambda b, r: (b, 0, 0)),
            pl.BlockSpec((1, Cn, W), lambda b, r: (b, 0, 0)),
            pl.BlockSpec((1, Cn, W), lambda b, r: (b, 0, 0)),
            pl.BlockSpec((1, NA_H, GRID_W, nwin), bias_idx),
        ],
        out_specs=pl.BlockSpec((1, GRID_W, W), lambda b, r: (b, r, 0)),
        out_shape=jax.ShapeDtypeStruct((B, S, W), F32),
        compiler_params=_cparams("parallel", "arbitrary"),
        name="na_attn",
    )(q, k, v, kc, vc, bias_tab)


def _ctx_attn_kernel(q_ref, k_ref, v_ref, o_ref):
    for h in range(NA_H):
        sl = slice(h * HEAD_DIM, (h + 1) * HEAD_DIM)
        s = _mm(q_ref[0, :, sl], k_ref[0, :, sl], nt=True)
        o_ref[0, :, sl] = _softmax_av([s], [v_ref[0, :, sl]])


def _ctx_attention(q, k, v):
    B, Cn, W = q.shape
    spec = pl.BlockSpec((1, Cn, W), lambda b: (b, 0, 0))
    return pl.pallas_call(
        _ctx_attn_kernel,
        grid=(B,),
        in_specs=[spec] * 3,
        out_specs=spec,
        out_shape=jax.ShapeDtypeStruct((B, Cn, W), F32),
        compiler_params=_cparams("parallel"),
        name="ctx_attn",
    )(q, k, v)


def _conv_kernel(z_ref, w_ref, o_ref):
    T = z_ref.shape[1]
    z = z_ref[0]
    zb, zc, zx = z[:, :SC_W], z[:, SC_W:2 * SC_W], z[:, 2 * SC_W:]
    u = zc * zx
    t = lax.broadcasted_iota(jnp.int32, u.shape, 0)
    prev = jnp.where(t == 0, 0.0, pltpu.roll(u, 1, 0))
    nxt = jnp.where(t == T - 1, 0.0, pltpu.roll(u, T - 1, 0))
    o_ref[0] = zb * (prev * w_ref[0:1, :] + u * w_ref[1:2, :] + nxt * w_ref[2:3, :])


def _short_conv(zsc, w):
    B, T, _ = zsc.shape
    return pl.pallas_call(
        _conv_kernel,
        grid=(B,),
        in_specs=[pl.BlockSpec((1, T, 3 * SC_W), lambda b: (b, 0, 0)),
                  pl.BlockSpec((3, SC_W), lambda b: (0, 0))],
        out_specs=pl.BlockSpec((1, T, SC_W), lambda b: (b, 0, 0)),
        out_shape=jax.ShapeDtypeStruct((B, T, SC_W), F32),
        compiler_params=_cparams("parallel"),
        name="short_conv",
    )(zsc, w)


def _proj_out_kernel(yrw_ref, yna_ref, ysc_ref, w_ref, x_ref, mod_ref, o_ref):
    a, b = RW_W, RW_W + NA_W
    y = (jnp.dot(yrw_ref[0].astype(BF16), w_ref[:a, :], preferred_element_type=F32)
         + jnp.dot(yna_ref[0].astype(BF16), w_ref[a:b, :], preferred_element_type=F32)
         + jnp.dot(ysc_ref[0].astype(BF16), w_ref[b:, :], preferred_element_type=F32))
    o_ref[0] = x_ref[0] + mod_ref[0, 2] * y


def _proj_out(yrw, yna, ysc, w_bf, x, mod):
    B, T, D = x.shape
    tm = 256
    row = lambda b, t: (b, t, 0)
    return pl.pallas_call(
        _proj_out_kernel,
        grid=(B, T // tm),
        in_specs=[
            pl.BlockSpec((1, tm, RW_W), row),
            pl.BlockSpec((1, tm, NA_W), row),
            pl.BlockSpec((1, tm, SC_W), row),
            pl.BlockSpec(w_bf.shape, lambda b, t: (0, 0)),
            pl.BlockSpec((1, tm, D), row),
            pl.BlockSpec((1, 6, 1, D), lambda b, t: (b, 0, 0, 0)),
        ],
        out_specs=pl.BlockSpec((1, tm, D), row),
        out_shape=jax.ShapeDtypeStruct((B, T, D), F32),
        compiler_params=_cparams("parallel", "parallel"),
        name="proj_out",
    )(yrw, yna, ysc, w_bf, x, mod)


PEER_P_ROUTE = 3


def _peer_query_kernel(x_ref, mod_ref, g_ref, wh_ref, wl_ref, sk_ref, ht_ref, s_ref):
    h = _norm_mod(x_ref[0], g_ref[...], mod_ref[0, 4], mod_ref[0, 3])
    h_hi = h.astype(BF16)
    ht_ref[0] = h.T.astype(BF16)
    q = jnp.dot(h_hi, wh_ref[...], preferred_element_type=F32)
    if PEER_P_ROUTE == 3:
        h_lo = (h - h_hi.astype(F32)).astype(BF16)
        q = q + jnp.dot(h_lo, wh_ref[...], preferred_element_type=F32) \
            + jnp.dot(h_hi, wl_ref[...], preferred_element_type=F32)
    dq = sk_ref.shape[2]
    for hp in range(sk_ref.shape[0]):
        s_ref[0, hp] = _mm(sk_ref[hp], q[:, hp * dq:(hp + 1) * dq], PEER_P_ROUTE, nt=True)


def _peer_query(x, mod, g, qw_hi, qw_lo, sub_keys):
    B, T, D = x.shape
    tm = 256
    nq = qw_hi.shape[1]
    sk = sub_keys.reshape(2 * PEER_H, PEER_NKEYS, -1)
    return pl.pallas_call(
        _peer_query_kernel,
        grid=(B, T // tm),
        in_specs=[
            pl.BlockSpec((1, tm, D), lambda b, t: (b, t, 0)),
            pl.BlockSpec((1, 6, 1, D), lambda b, t: (b, 0, 0, 0)),
            pl.BlockSpec((1, D), lambda b, t: (0, 0)),
            pl.BlockSpec((D, nq), lambda b, t: (0, 0)),
            pl.BlockSpec((D, nq), lambda b, t: (0, 0)),
            pl.BlockSpec(sk.shape, lambda b, t: (0, 0, 0)),
        ],
        out_specs=[pl.BlockSpec((1, D, tm), lambda b, t: (b, 0, t)),
                   pl.BlockSpec((1, 2 * PEER_H, PEER_NKEYS, tm), lambda b, t: (b, 0, 0, t))],
        out_shape=[jax.ShapeDtypeStruct((B, D, T), BF16),
                   jax.ShapeDtypeStruct((B, 2 * PEER_H, PEER_NKEYS, T), F32)],
        compiler_params=_cparams("parallel", "parallel"),
        name="peer_query",
    )(x, mod, g.reshape(1, D), qw_hi, qw_lo, sk)


def _extract_top(s, n, on_pick):
    rows = lax.broadcasted_iota(jnp.int32, s.shape, 0).astype(F32)
    big = float(s.shape[0])
    for it in range(n):
        m = jnp.max(s, axis=0, keepdims=True)
        idx = jnp.min(jnp.where(s == m, rows, big), axis=0, keepdims=True)
        sel = rows == idx
        on_pick(it, m, sel)
        s = jnp.where(sel, -jnp.inf, s)


def _dup_bf16(x):
    bits = lax.bitcast_convert_type(x.astype(BF16).astype(F32), jnp.uint32)
    return bits | (bits >> 16)


def _peer_route_kernel(s_ref, rank2_ref, cnt_ref, e1_ref, e2_ref, v1_scr, v2_scr, cand_scr):
    tn = s_ref.shape[3]
    for h in range(PEER_H):
        s1 = s_ref[0, 2 * h]
        s2 = s_ref[0, 2 * h + 1]
        ranks = []
        for s, scr in ((s1, v1_scr), (s2, v2_scr)):
            rank = [jnp.full(s.shape, float(PEER_NKEYS), F32)]

            def pick(it, m, sel, scr=scr, rank=rank):
                scr[it:it + 1, :] = m
                rank[0] = jnp.where(sel, float(it), rank[0])

            _extract_top(s, PEER_TOPK, pick)
            ranks.append(rank[0])
        rank1, rank2 = ranks
        cand_scr[...] = jnp.full(cand_scr.shape, -jnp.inf, F32)
        for ci, (a, b) in enumerate(_CAND):
            cand_scr[ci:ci + 1, :] = v1_scr[a:a + 1, :] + v2_scr[b:b + 1, :]
        picked = [jnp.zeros(cand_scr.shape, F32), jnp.zeros((1, tn), F32), None]

        def pick_c(it, m, sel, picked=picked):
            if it == 0:
                picked[2] = m
            picked[0] = jnp.where(sel, 1.0, picked[0])
            picked[1] = picked[1] + jnp.exp(m - picked[2])

        _extract_top(cand_scr[...], PEER_TOPK, pick_c)
        selmask, z = picked[0], picked[1]
        ca = lax.broadcasted_iota(jnp.int32, (PEER_TOPK, _NCAND_PAD), 0)
        cj = lax.broadcasted_iota(jnp.int32, (PEER_TOPK, _NCAND_PAD), 1)
        row_of = functools.reduce(jnp.logical_or, [(ca == a) & (cj == ci) for ci, (a, _) in enumerate(_CAND)])
        cnt_a = jnp.dot(row_of.astype(BF16), selmask.astype(BF16), preferred_element_type=F32)
        cnt = jnp.zeros(s1.shape, F32)
        for a in range(PEER_TOPK):
            cnt = jnp.where(rank1 == float(a), cnt_a[a:a + 1, :], cnt)
        rank2_ref[0, h] = rank2.astype(BF16)
        cnt_ref[0, h] = _dup_bf16(cnt)
        e1_ref[0, h] = _dup_bf16(jnp.exp(s1 - v1_scr[0:1, :]) / z)
        e2_ref[0, h] = jnp.exp(s2 - v2_scr[0:1, :]).astype(BF16)


def _peer_route(sT):
    B, _, NK, T = sT.shape
    tn = 256
    ospec = pl.BlockSpec((1, PEER_H, NK, tn), lambda b, t: (b, 0, 0, t))
    oshape = lambda dt: jax.ShapeDtypeStruct((B, PEER_H, NK, T), dt)
    return pl.pallas_call(
        _peer_route_kernel,
        grid=(B, T // tn),
        in_specs=[pl.BlockSpec((1, 2 * PEER_H, NK, tn), lambda b, t: (b, 0, 0, t))],
        out_specs=[ospec] * 4,
        out_shape=[oshape(BF16), oshape(jnp.uint32), oshape(jnp.uint32), oshape(BF16)],
        scratch_shapes=[pltpu.VMEM((PEER_TOPK, tn), F32), pltpu.VMEM((PEER_TOPK, tn), F32),
                        pltpu.VMEM((_NCAND_PAD, tn), F32)],
        compiler_params=_cparams("parallel", "parallel"),
        name="peer_route",
    )(sT)


_GELU_C = float(np.sqrt(2.0 / np.pi))
PEER_SUB = 256


def _gelu_tanh(x):
    hx = 0.5 * x
    return hx + hx * jnp.tanh(x * (_GELU_C + (_GELU_C * 0.044715) * (x * x)))


def _row_tile(words):
    return pltpu.bitcast(jnp.broadcast_to(words, (PEER_NKEYS // 2, words.shape[1])), BF16)


def _peer_expert_kernel(ht_ref, rank2_ref, cnt_ref, e1_ref, e2_ref, u_ref, vt_ref, x_ref, mod_ref,
                        o_ref, acc_ref, g_scr, *, te):
    e = pl.program_id(2)

    @pl.when(e == 0)
    def _():
        acc_ref[...] = jnp.zeros(acc_ref.shape, F32)

    ht = ht_ref[0]
    rows = te // PEER_NKEYS
    for ii in range(rows):
        i = e * rows + ii
        g = None
        for hd in range(PEER_H):
            cnt_row = _row_tile(cnt_ref[0, hd, pl.ds(i, 1), :])
            e1_row = _row_tile(e1_ref[0, hd, pl.ds(i, 1), :])
            gh = jnp.where(rank2_ref[0, hd] < cnt_row, e2_ref[0, hd] * e1_row, jnp.zeros((), BF16))
            g = gh if g is None else g + gh
        g_scr[ii * PEER_NKEYS:(ii + 1) * PEER_NKEYS, :] = g
    for sb in range(te // PEER_SUB):
        es = slice(sb * PEER_SUB, (sb + 1) * PEER_SUB)
        a = jnp.dot(u_ref[es, :], ht, preferred_element_type=F32)
        g_scr[es, :] = _gelu_tanh(a.astype(BF16)) * g_scr[es, :]
    acc_ref[...] += jnp.dot(vt_ref[...], g_scr[...], preferred_element_type=F32)

    @pl.when(e == pl.num_programs(2) - 1)
    def _():
        o_ref[0] = x_ref[0] + mod_ref[0, 5] * acc_ref[...].T


def _peer_expert(h2, route, u_bf, vt_bf, x, mod):
    B, T, D = x.shape
    tn = min(512, T)
    te = 1024
    ne = u_bf.shape[0] // te
    rspec = pl.BlockSpec((1, PEER_H, PEER_NKEYS, tn), lambda b, t, e: (b, 0, 0, t))
    return pl.pallas_call(
        functools.partial(_peer_expert_kernel, te=te),
        grid=(B, T // tn, ne),
        in_specs=[
            pl.BlockSpec((1, D, tn), lambda b, t, e: (b, 0, t)),
            rspec, rspec, rspec, rspec,
            pl.BlockSpec((te, D), lambda b, t, e: (e, 0)),
            pl.BlockSpec((D, te), lambda b, t, e: (0, e)),
            pl.BlockSpec((1, tn, D), lambda b, t, e: (b, t, 0)),
            pl.BlockSpec((1, 6, 1, D), lambda b, t, e: (b, 0, 0, 0)),
        ],
        out_specs=pl.BlockSpec((1, tn, D), lambda b, t, e: (b, t, 0)),
        out_shape=jax.ShapeDtypeStruct((B, T, D), F32),
        scratch_shapes=[pltpu.VMEM((D, tn), F32), pltpu.VMEM((te, tn), BF16)],
        compiler_params=_cparams("parallel", "parallel", "arbitrary"),
        name="peer_expert",
    )(h2, *route, u_bf, vt_bf, x, mod)


def _peer(x, mod, g, qw_hi, qw_lo, sub_keys, u_bf, vt_bf):
    h2, sT = _peer_query(x, mod, g, qw_hi, qw_lo, sub_keys)
    route = _peer_route(sT)
    return _peer_expert(h2, route, u_bf, vt_bf, x, mod)


def _final_norm_kernel(x_ref, g_ref, o_ref):
    x = x_ref[0]
    ms = jnp.mean(x * x, axis=-1, keepdims=True)
    o_ref[0] = x * lax.rsqrt(ms + NORM_EPS) * g_ref[...]


def _final_norm(x, g):
    B, T, D = x.shape
    tm = 512
    return pl.pallas_call(
        _final_norm_kernel,
        grid=(B, T // tm),
        in_specs=[pl.BlockSpec((1, tm, D), lambda b, t: (b, t, 0)),
                  pl.BlockSpec((1, D), lambda b, t: (0, 0))],
        out_specs=pl.BlockSpec((1, tm, D), lambda b, t: (b, t, 0)),
        out_shape=jax.ShapeDtypeStruct((B, T, D), F32),
        compiler_params=_cparams("parallel", "parallel"),
        name="final_norm",
    )(x, g.reshape(1, D))


def _split_bf16(w):
    hi = w.astype(BF16)
    return hi, (w - hi.astype(F32)).astype(BF16)


def kernel(x, c, ctx, c_ctx, ada_w, ada_b, norm1_g, norm2_g, w_in, rw_w0, rw_w_up, rw_a0, rw_a_up, rw_g_up,
           rw_k_k, rw_k_a, rw_r_k, rw_lnx_g, na_rpb, sc_conv_w, w_out, peer_q_w, peer_sub_keys, peer_u,
           peer_v, final_g):
    B, S, D = x.shape
    depth = ada_w.shape[0]
    rows = S // GRID_W
    kr = min(NA_KR, rows)
    n_mod = -(-(B + 1) // 8) * 8
    cc = jnp.concatenate([c, c_ctx[None, :], jnp.zeros((n_mod - B - 1, D), F32)], axis=0)
    mod = _ada(cc, ada_w, ada_b)
    xl, xc = x, ctx
    for i in range(depth):
        need_ctx = i < depth - 1
        mod_l = mod[i, :B].reshape(B, 6, 1, D)
        mod_c = jnp.broadcast_to(mod[i, B].reshape(1, 6, 1, D), (B, 6, 1, D))
        w_in_bf = w_in[i].astype(BF16)
        w_out_bf = w_out[i].astype(BF16)
        rw = (rw_w0[i], rw_w_up[i], rw_a0[i], rw_a_up[i], rw_k_k[i], rw_k_a[i])

        zrw_l, q_l, k_l, v_l, zsc_l = _proj_in(xl, mod_l, norm1_g[i], w_in_bf)
        zrw_c, q_c, k_c, v_c, zsc_c = _proj_in(xc, mod_c, norm1_g[i], w_in_bf)
        y2_c, y2_l = _rwkv_state(_rwkv_prep(zrw_c, *rw), _rwkv_prep(zrw_l, *rw))
        yrw_l = _rwkv_out(y2_l, zrw_l, rw_g_up[i], rw_r_k[i], rw_lnx_g[i])
        yna_l = _na_attention(q_l, k_l, v_l, k_c, v_c, _na_bias_table(na_rpb[i], rows, kr))
        ysc_l = _short_conv(zsc_l, sc_conv_w[i])
        xl = _proj_out(yrw_l, yna_l, ysc_l, w_out_bf, xl, mod_l)

        qw_hi, qw_lo = _split_bf16(peer_q_w[i])
        u_bf = peer_u[i].astype(BF16)
        vt_bf = peer_v[i].astype(BF16).T
        peer_args = (norm2_g[i], qw_hi, qw_lo, peer_sub_keys[i], u_bf, vt_bf)
        xl = _peer(xl, mod_l, *peer_args)
        if need_ctx:
            yrw_c = _rwkv_out(y2_c, zrw_c, rw_g_up[i], rw_r_k[i], rw_lnx_g[i])
            yna_c = _ctx_attention(q_c, k_c, v_c)
            ysc_c = _short_conv(zsc_c, sc_conv_w[i])
            xc = _proj_out(yrw_c, yna_c, ysc_c, w_out_bf, xc, mod_c)
            xc = _peer(xc, mod_c, *peer_args)
    return _final_norm(xl, final_g)
```

```python
import functools

import numpy as np
import jax
import jax.numpy as jnp
from jax import lax
from jax.experimental import pallas as pl
from jax.experimental.pallas import tpu as pltpu

F32 = jnp.float32
BF16 = jnp.bfloat16

HEAD_DIM = 64
GRID_W = 64
RW_H = 4
RW_W = RW_H * HEAD_DIM
RW_LW = 64
RW_LA = 64
RW_LG = 128
RW_COLS = 3 * RW_W + 2 * RW_LW + 2 * RW_LA + RW_LG
NA_H = 8
NA_W = NA_H * HEAD_DIM
NA_KR = 8
NA_KC = 16
SC_W = 256
PEER_H = 8
PEER_NKEYS = 128
PEER_TOPK = 16
NORM_EPS = 1e-6
GN_EPS = 64e-5
NEG_INF = -1e30
RW_CHUNK = 64

VMEM_LIMIT = 52 * 1024 * 1024

_CAND = [(a, b) for a in range(PEER_TOPK) for b in range(PEER_TOPK) if (a + 1) * (b + 1) <= PEER_TOPK]
_NCAND = len(_CAND)
_NCAND_PAD = -(-_NCAND // 8) * 8


def _cparams(*sem):
    return pltpu.CompilerParams(dimension_semantics=sem, vmem_limit_bytes=VMEM_LIMIT)


def _mm(a, b, passes=1, nt=False):
    dims = (((1,), (1,)), ((), ())) if nt else (((1,), (0,)), ((), ()))
    dg = functools.partial(lax.dot_general, dimension_numbers=dims, preferred_element_type=F32)
    if passes == 6:
        return dg(a.astype(F32), b.astype(F32), precision=lax.Precision.HIGHEST)
    a_hi = a.astype(BF16)
    b_hi = b.astype(BF16)
    out = dg(a_hi, b_hi)
    if passes == 3:
        a_lo = (a - a_hi.astype(F32)).astype(BF16)
        b_lo = (b - b_hi.astype(F32)).astype(BF16)
        out = out + dg(a_lo, b_hi) + dg(a_hi, b_lo)
    return out


def _sigmoid(x):
    return 1.0 / (1.0 + jnp.exp(-x))


def _norm_mod(x, g, scale, shift):
    ms = jnp.mean(x * x, axis=-1, keepdims=True)
    y = x * lax.rsqrt(ms + NORM_EPS) * g
    return y * (1.0 + scale) + shift


def _ada_kernel(c_ref, w_ref, b_ref, o_ref):
    cc = c_ref[...]
    o_ref[0] = _mm(cc * _sigmoid(cc), w_ref[0], 6) + b_ref[0]


def _ada(cc, ada_w, ada_b):
    L, D, D6 = ada_w.shape
    R = cc.shape[0]
    tn = 1536
    return pl.pallas_call(
        _ada_kernel,
        grid=(L, D6 // tn),
        in_specs=[
            pl.BlockSpec((R, D), lambda l, j: (0, 0)),
            pl.BlockSpec((1, D, tn), lambda l, j: (l, 0, j)),
            pl.BlockSpec((1, 1, tn), lambda l, j: (l, 0, j)),
        ],
        out_specs=pl.BlockSpec((1, R, tn), lambda l, j: (l, 0, j)),
        out_shape=jax.ShapeDtypeStruct((L, R, D6), F32),
        compiler_params=_cparams("parallel", "parallel"),
        name="ada",
    )(cc, ada_w, ada_b.reshape(L, 1, D6))


def _proj_in_kernel(x_ref, mod_ref, g_ref, w_ref, zrw_ref, q_ref, k_ref, v_ref, zsc_ref):
    h = _norm_mod(x_ref[0], g_ref[...], mod_ref[0, 1], mod_ref[0, 0])
    z = jnp.dot(h.astype(BF16), w_ref[...], preferred_element_type=F32)
    o = RW_COLS
    zrw_ref[0] = z[:, :o]
    q_ref[0] = (z[:, o:o + NA_W] * (HEAD_DIM ** -0.5)).astype(BF16)
    k_ref[0] = z[:, o + NA_W:o + 2 * NA_W].astype(BF16)
    v_ref[0] = z[:, o + 2 * NA_W:o + 3 * NA_W].astype(BF16)
    zsc_ref[0] = z[:, o + 3 * NA_W:]


def _proj_in(x, mod, g, w_bf):
    B, T, D = x.shape
    tm = 256
    n_out = w_bf.shape[1]
    row = lambda b, t: (b, t, 0)
    return pl.pallas_call(
        _proj_in_kernel,
        grid=(B, T // tm),
        in_specs=[
            pl.BlockSpec((1, tm, D), row),
            pl.BlockSpec((1, 6, 1, D), lambda b, t: (b, 0, 0, 0)),
            pl.BlockSpec((1, D), lambda b, t: (0, 0)),
            pl.BlockSpec((D, n_out), lambda b, t: (0, 0)),
        ],
        out_specs=[
            pl.BlockSpec((1, tm, RW_COLS), row),
            pl.BlockSpec((1, tm, NA_W), row),
            pl.BlockSpec((1, tm, NA_W), row),
            pl.BlockSpec((1, tm, NA_W), row),
            pl.BlockSpec((1, tm, 3 * SC_W), row),
        ],
        out_shape=[
            jax.ShapeDtypeStruct((B, T, RW_COLS), F32),
            jax.ShapeDtypeStruct((B, T, NA_W), BF16),
            jax.ShapeDtypeStruct((B, T, NA_W), BF16),
            jax.ShapeDtypeStruct((B, T, NA_W), BF16),
            jax.ShapeDtypeStruct((B, T, 3 * SC_W), F32),
        ],
        compiler_params=_cparams("parallel", "parallel"),
        name="proj_in",
    )(x, mod, g.reshape(1, D), w_bf)


RW_CHUNKS_PER_STEP = 2
RW_P_ALG = 1
RW_P_LORA = 3


def _rwkv_prep_kernel(z_ref, w0_ref, wup_ref, a0_ref, aup_ref, kk_ref, ka_ref,
                      phi_ref, zc_ref, wr_ref, yin_ref):
    C = RW_CHUNK
    W = RW_W
    assert C == HEAD_DIM
    N = RW_H * C
    row = lax.broadcasted_iota(jnp.int32, (C, C), 0)
    col = lax.broadcasted_iota(jnp.int32, (C, C), 1)
    brow = lax.broadcasted_iota(jnp.int32, (N, W), 0)
    bcol = lax.broadcasted_iota(jnp.int32, (N, W), 1)
    bd = (brow // C) == (bcol // HEAD_DIM)
    eye_j = lax.broadcasted_iota(jnp.int32, (HEAD_DIM, W), 0) == \
        lax.broadcasted_iota(jnp.int32, (HEAD_DIM, W), 1) % HEAD_DIM
    head_ones = bd.astype(F32)
    mm = functools.partial(_mm, passes=RW_P_ALG)
    mm_tn = lambda a, b: lax.dot_general(a.astype(BF16), b.astype(BF16), (((0,), (0,)), ((), ())),
                                         preferred_element_type=F32)
    collapse = lambda m: functools.reduce(lambda a, b: a + b, [m[h * C:(h + 1) * C] for h in range(RW_H)])

    def direction(ci, d):
        z = z_ref[0, ci * C:(ci + 1) * C, :]
        r = z[:, 0:W]
        k = z[:, W:2 * W]
        v = z[:, 2 * W:3 * W]
        zw = z[:, 3 * W:3 * W + 2 * RW_LW]
        za = z[:, 3 * W + 2 * RW_LW:3 * W + 2 * RW_LW + 2 * RW_LA]
        kk = k * kk_ref[...]
        kk = kk * lax.rsqrt(_mm(kk * kk, head_ones, 3) + 1e-12)
        lw = jnp.tanh(zw[:, d * RW_LW:(d + 1) * RW_LW])
        la = za[:, d * RW_LA:(d + 1) * RW_LA]
        w_raw = w0_ref[d:d + 1, :] + _mm(lw, wup_ref[d], RW_P_LORA)
        softplus = jnp.maximum(-w_raw, 0.0) + jnp.log(1.0 + jnp.exp(-jnp.abs(w_raw)))
        logw = -jnp.exp(-softplus - 0.5)
        a = _sigmoid(a0_ref[d:d + 1, :] + _mm(la, aup_ref[d], RW_P_LORA))
        k_d = k * (1.0 + (a - 1.0) * ka_ref[...])
        b_d = kk * a
        if d == 0:
            tri_incl, tri_strict = col <= row, col < row
        else:
            tri_incl, tri_strict = col >= row, col > row
        cum = _mm(tri_incl.astype(F32), logw, 6)
        tot = jnp.sum(logw, axis=0, keepdims=True)
        e_neg = jnp.exp(-cum)
        e_end = jnp.exp(tot - cum)
        a_t = -kk * jnp.exp(cum - logw)
        r_t = r * jnp.exp(cum)
        b_t = b_d * e_neg
        k_t = k_d * e_neg
        b_p = b_d * e_end
        k_p = k_d * e_end
        p_c = jnp.exp(tot)
        bdx = lambda x: jnp.where(bd, jnp.concatenate([x] * RW_H, axis=0), 0.0)
        a_bd, r_bd, v_bd = bdx(a_t), bdx(r_t), bdx(v)
        yield
        p4 = mm(jnp.concatenate([a_bd, r_bd], axis=0),
                jnp.concatenate([bdx(b_t), bdx(k_t)], axis=0), nt=True)
        yield
        tri_strict_bd = bd & ((bcol < brow) if d == 0 else (bcol > brow))
        tri_incl_bd = bd & ((bcol <= brow) if d == 0 else (bcol >= brow))
        l_ab = jnp.where(tri_strict_bd, p4[:N, :N], 0.0)
        l_ak = jnp.where(tri_strict_bd, p4[:N, N:], 0.0)
        m_rb = jnp.where(tri_incl_bd, p4[N:, :N], 0.0)
        m_rk = jnp.where(tri_incl_bd, p4[N:, N:], 0.0)
        t_inv = jnp.where(brow == bcol, 1.0, l_ab)
        pw = l_ab
        n = 2
        lakv = mm(l_ak, v_bd)
        mrkv = mm(m_rk, v_bd)
        kpv = mm_tn(bdx(k_p), v_bd)
        while n < C:
            pw = mm(pw, pw)
            yield
            t_inv = t_inv + mm(t_inv, pw)
            n *= 2
        yield
        x = mm(t_inv, jnp.concatenate([a_bd, lakv], axis=1))
        yield
        wy = mm(m_rb, x)
        pz = mm_tn(bdx(b_p), x)
        wr_ref[0, ci, d] = r_t + collapse(wy[:, :W])
        yin_ref[0, ci, d] = collapse(wy[:, W:] + mrkv)
        phi_ref[0, ci, d] = jnp.where(eye_j, p_c, 0.0) + collapse(pz[:, :W])
        zc_ref[0, ci, d] = collapse(pz[:, W:] + kpv)

    live = [direction(ci, d) for ci in range(z_ref.shape[1] // C) for d in range(2)]
    done = object()
    while live:
        live = [g for g in live if next(g, done) is not done]


def _rwkv_prep(zrw, w0, w_up, a0, a_up, k_k, k_a):
    B, T, _ = zrw.shape
    C = RW_CHUNK
    nch = T // C
    full = lambda a: pl.BlockSpec(a.shape, lambda b, c: (0,) * a.ndim)
    k_k = k_k.reshape(1, RW_W)
    k_a = k_a.reshape(1, RW_W)
    cps = RW_CHUNKS_PER_STEP
    ospec = pl.BlockSpec((1, cps, 2, HEAD_DIM, RW_W), lambda b, c: (b, c, 0, 0, 0))
    oshape = jax.ShapeDtypeStruct((B, nch, 2, HEAD_DIM, RW_W), F32)
    return pl.pallas_call(
        _rwkv_prep_kernel,
        grid=(B, nch // cps),
        in_specs=[pl.BlockSpec((1, cps * C, RW_COLS), lambda b, c: (b, c, 0)),
                  full(w0), full(w_up), full(a0), full(a_up), full(k_k), full(k_a)],
        out_specs=[ospec] * 4,
        out_shape=[oshape] * 4,
        compiler_params=_cparams("parallel", "parallel"),
        name="rwkv_prep",
    )(zrw, w0, w_up, a0, a_up, k_k, k_a)


RW_P_STATE = 3


def _rwkv_state_kernel(phic, zcc, wrc, yinc, phil, zcl, wrl, yinl, yc_ref, yl_ref):
    d = pl.program_id(1)
    mm = functools.partial(_mm, passes=RW_P_STATE)

    N = RW_H * HEAD_DIM
    bd = (lax.broadcasted_iota(jnp.int32, (N, RW_W), 0) // HEAD_DIM) == \
        (lax.broadcasted_iota(jnp.int32, (N, RW_W), 1) // HEAD_DIM)

    def run(H, phi, zc, wr, yin, y_ref):
        nch = phi.shape[1]

        def body(s, H):
            c = jnp.where(d == 0, s, nch - 1 - s)
            t0 = pl.multiple_of(c * RW_CHUNK, RW_CHUNK)
            h_bd = jnp.where(bd, jnp.concatenate([H] * RW_H, axis=0), 0.0)
            ops = jnp.concatenate([wr[0, c, 0], phi[0, c, 0]], axis=0)
            out = mm(ops, h_bd) + jnp.concatenate([yin[0, c, 0], zc[0, c, 0]], axis=0)
            y_ref[0, 0, pl.ds(t0, RW_CHUNK), :] = out[:RW_CHUNK]
            return out[RW_CHUNK:]

        return lax.fori_loop(0, nch, body, H)

    H = jnp.zeros((HEAD_DIM, RW_W), F32)
    H = run(H, phic, zcc, wrc, yinc, yc_ref)
    run(H, phil, zcl, wrl, yinl, yl_ref)


def _rwkv_state(ops_c, ops_l):
    B, nc = ops_c[0].shape[:2]
    nl = ops_l[0].shape[1]
    spec = lambda n: pl.BlockSpec((1, n, 1, HEAD_DIM, RW_W), lambda b, d: (b, 0, d, 0, 0))
    yspec = lambda n: pl.BlockSpec((1, 1, n * RW_CHUNK, RW_W), lambda b, d: (b, d, 0, 0))
    return pl.pallas_call(
        _rwkv_state_kernel,
        grid=(B, 2),
        in_specs=[spec(nc)] * 4 + [spec(nl)] * 4,
        out_specs=[yspec(nc), yspec(nl)],
        out_shape=[jax.ShapeDtypeStruct((B, 2, nc * RW_CHUNK, RW_W), F32),
                   jax.ShapeDtypeStruct((B, 2, nl * RW_CHUNK, RW_W), F32)],
        compiler_params=_cparams("parallel", "parallel"),
        name="rwkv_state",
    )(*ops_c, *ops_l)


def _rwkv_out_kernel(yf_ref, yr_ref, z_ref, gup_ref, rk_ref, lnx_ref, o_ref):
    W = RW_W
    y = yf_ref[0, 0] + yr_ref[0, 0]
    z = z_ref[0]
    r, k, v = z[:, 0:W], z[:, W:2 * W], z[:, 2 * W:3 * W]
    zg = z[:, RW_COLS - RW_LG:RW_COLS]
    hrow = lax.broadcasted_iota(jnp.int32, (W, W), 0) // HEAD_DIM
    hcol = lax.broadcasted_iota(jnp.int32, (W, W), 1) // HEAD_DIM
    head_ones = (hrow == hcol).astype(F32)
    inv = 1.0 / HEAD_DIM
    mu = _mm(y, head_ones, 3) * inv
    yc = y - mu
    var = _mm(yc * yc, head_ones, 3) * inv
    yn = yc * lax.rsqrt(var + GN_EPS) * lnx_ref[...]
    bonus = _mm(r * k * rk_ref[...], head_ones, 3)
    yn = yn + bonus * v
    gate = _mm(_sigmoid(zg), gup_ref[...], 3)
    o_ref[0] = yn * gate


def _rwkv_out(y2, zrw, g_up, r_k, lnx_g):
    B, _, T, W = y2.shape
    tm = 256
    return pl.pallas_call(
        _rwkv_out_kernel,
        grid=(B, T // tm),
        in_specs=[
            pl.BlockSpec((1, 1, tm, W), lambda b, t: (b, 0, t, 0)),
            pl.BlockSpec((1, 1, tm, W), lambda b, t: (b, 1, t, 0)),
            pl.BlockSpec((1, tm, RW_COLS), lambda b, t: (b, t, 0)),
            pl.BlockSpec((RW_LG, W), lambda b, t: (0, 0)),
            pl.BlockSpec((1, W), lambda b, t: (0, 0)),
            pl.BlockSpec((1, W), lambda b, t: (0, 0)),
        ],
        out_specs=pl.BlockSpec((1, tm, W), lambda b, t: (b, t, 0)),
        out_shape=jax.ShapeDtypeStruct((B, T, W), F32),
        compiler_params=_cparams("parallel", "parallel"),
        name="rwkv_out",
    )(y2, y2, zrw, g_up, r_k.reshape(1, W), lnx_g.reshape(1, W))


def _softmax_av_steps(scores, values, store):
    m = functools.reduce(jnp.maximum, [jnp.max(s, axis=-1, keepdims=True) for s in scores])
    yield
    ps = [jnp.exp(s - m) for s in scores]
    l = functools.reduce(lambda a, b: a + b, [jnp.sum(p, axis=-1, keepdims=True) for p in ps])
    yield
    o = functools.reduce(lambda a, b: a + b,
                         [jnp.dot(p.astype(BF16), v, preferred_element_type=F32) for p, v in zip(ps, values)])
    yield
    store(o / l)


def _interleave(chains):
    live = list(chains)
    done = object()
    while live:
        live = [g for g in live if next(g, done) is not done]


NA_HEADS_INTERLEAVED = 8


def _na_kernel(q_ref, k_ref, v_ref, kc_ref, vc_ref, bias_ref, o_ref, *, kr):
    r = pl.program_id(1)
    rows = pl.num_programs(1)
    start = pl.multiple_of(jnp.clip(r - NA_KR // 2, 0, rows - kr) * GRID_W, GRID_W)
    nwin = kr * GRID_W

    def head(h):
        sl = slice(h * HEAD_DIM, (h + 1) * HEAD_DIM)
        q = q_ref[0, :, sl]
        kb = k_ref[0, pl.ds(start, nwin), sl]
        vb = v_ref[0, pl.ds(start, nwin), sl]
        s_win = _mm(q, kb, nt=True) + bias_ref[0, h]
        s_ctx = _mm(q, kc_ref[0, :, sl], nt=True)
        yield

        def store(o):
            o_ref[0, :, sl] = o

        yield from _softmax_av_steps([s_win, s_ctx], [vb, vc_ref[0, :, sl]], store)

    for h0 in range(0, NA_H, NA_HEADS_INTERLEAVED):
        _interleave(head(h) for h in range(h0, h0 + NA_HEADS_INTERLEAVED))


def _na_bias_table(rpb, rows, kr):
    ci = np.arange(GRID_W)
    c0 = np.clip(ci - NA_KC // 2, 0, GRID_W - NA_KC)
    col_ok = (ci[None, :] >= c0[:, None]) & (ci[None, :] < c0[:, None] + NA_KC)
    dc = np.clip(ci[None, :] - ci[:, None] + (NA_KC - 1), 0, 2 * NA_KC - 2)
    n_off = NA_KR
    dr = np.arange(kr)[None, :] - np.arange(n_off)[:, None] + (NA_KR - 1)
    dr_ok = (dr >= 0) & (dr < 2 * NA_KR - 1)
    sel_r = (dr[:, :, None] == np.arange(2 * NA_KR - 1)).astype(np.float32)
    sel_c = (dc[:, :, None] == np.arange(2 * NA_KC - 1)).astype(np.float32)
    bias = jnp.einsum('hrc,okr,qwc->ohqkw', rpb.astype(F32), sel_r, sel_c, precision=lax.Precision.HIGHEST)
    ok = dr_ok[:, None, None, :, None] & col_ok[None, None, :, None, :]
    bias = jnp.where(ok, bias, NEG_INF)
    return bias.reshape(n_off, NA_H, GRID_W, kr * GRID_W)


def _na_attention(q, k, v, kc, vc, bias_tab):
    B, S, W = q.shape
    Cn = kc.shape[1]
    rows = S // GRID_W
    kr = min(NA_KR, rows)
    nwin = kr * GRID_W

    def bias_idx(b, r):
        return (r - jnp.clip(r - NA_KR // 2, 0, rows - kr), 0, 0, 0)

    return pl.pallas_call(
        functools.partial(_na_kernel, kr=kr),
        grid=(B, rows),
        in_specs=[
            pl.BlockSpec((1, GRID_W, W), lambda b, r: (b, r, 0)),
            pl.BlockSpec((1, S, W), lambda b, r: (b, 0, 0)),
            pl.BlockSpec((1, S, W), lambda b, r: (b, 0, 0)),
            pl.BlockSpec((1, Cn, W), lambda b, r: (b, 0, 0)),
            pl.BlockSpec((1, Cn, W), lambda b, r: (b, 0, 0)),
            pl.BlockSpec((1, NA_H, GRID_W, nwin), bias_idx),
        ],
        out_specs=pl.BlockSpec((1, GRID_W, W), lambda b, r: (b, r, 0)),
        out_shape=jax.ShapeDtypeStruct((B, S, W), F32),
        compiler_params=_cparams("parallel", "arbitrary"),
        name="na_attn",
    )(q, k, v, kc, vc, bias_tab)


def _ctx_attn_kernel(q_ref, k_ref, v_ref, o_ref):
    def head(h):
        sl = slice(h * HEAD_DIM, (h + 1) * HEAD_DIM)
        s = _mm(q_ref[0, :, sl], k_ref[0, :, sl], nt=True)
        yield

        def store(o):
            o_ref[0, :, sl] = o

        yield from _softmax_av_steps([s], [v_ref[0, :, sl]], store)

    for h0 in range(0, NA_H, NA_HEADS_INTERLEAVED):
        _interleave(head(h) for h in range(h0, h0 + NA_HEADS_INTERLEAVED))


def _ctx_attention(q, k, v):
    B, Cn, W = q.shape
    spec = pl.BlockSpec((1, Cn, W), lambda b: (b, 0, 0))
    return pl.pallas_call(
        _ctx_attn_kernel,
        grid=(B,),
        in_specs=[spec] * 3,
        out_specs=spec,
        out_shape=jax.ShapeDtypeStruct((B, Cn, W), F32),
        compiler_params=_cparams("parallel"),
        name="ctx_attn",
    )(q, k, v)


def _conv_kernel(z_ref, w_ref, o_ref):
    T = z_ref.shape[1]
    z = z_ref[0]
    zb, zc, zx = z[:, :SC_W], z[:, SC_W:2 * SC_W], z[:, 2 * SC_W:]
    u = zc * zx
    t = lax.broadcasted_iota(jnp.int32, u.shape, 0)
    prev = jnp.where(t == 0, 0.0, pltpu.roll(u, 1, 0))
    nxt = jnp.where(t == T - 1, 0.0, pltpu.roll(u, T - 1, 0))
    o_ref[0] = zb * (prev * w_ref[0:1, :] + u * w_ref[1:2, :] + nxt * w_ref[2:3, :])


def _short_conv(zsc, w):
    B, T, _ = zsc.shape
    return pl.pallas_call(
        _conv_kernel,
        grid=(B,),
        in_specs=[pl.BlockSpec((1, T, 3 * SC_W), lambda b: (b, 0, 0)),
                  pl.BlockSpec((3, SC_W), lambda b: (0, 0))],
        out_specs=pl.BlockSpec((1, T, SC_W), lambda b: (b, 0, 0)),
        out_shape=jax.ShapeDtypeStruct((B, T, SC_W), F32),
        compiler_params=_cparams("parallel"),
        name="short_conv",
    )(zsc, w)


def _proj_out_kernel(yrw_ref, yna_ref, ysc_ref, w_ref, x_ref, mod_ref, o_ref):
    a, b = RW_W, RW_W + NA_W
    y = (jnp.dot(yrw_ref[0].astype(BF16), w_ref[:a, :], preferred_element_type=F32)
         + jnp.dot(yna_ref[0].astype(BF16), w_ref[a:b, :], preferred_element_type=F32)
         + jnp.dot(ysc_ref[0].astype(BF16), w_ref[b:, :], preferred_element_type=F32))
    o_ref[0] = x_ref[0] + mod_ref[0, 2] * y


def _proj_out(yrw, yna, ysc, w_bf, x, mod):
    B, T, D = x.shape
    tm = 256
    row = lambda b, t: (b, t, 0)
    return pl.pallas_call(
        _proj_out_kernel,
        grid=(B, T // tm),
        in_specs=[
            pl.BlockSpec((1, tm, RW_W), row),
            pl.BlockSpec((1, tm, NA_W), row),
            pl.BlockSpec((1, tm, SC_W), row),
            pl.BlockSpec(w_bf.shape, lambda b, t: (0, 0)),
            pl.BlockSpec((1, tm, D), row),
            pl.BlockSpec((1, 6, 1, D), lambda b, t: (b, 0, 0, 0)),
        ],
        out_specs=pl.BlockSpec((1, tm, D), row),
        out_shape=jax.ShapeDtypeStruct((B, T, D), F32),
        compiler_params=_cparams("parallel", "parallel"),
        name="proj_out",
    )(yrw, yna, ysc, w_bf, x, mod)


PEER_P_ROUTE = 3


def _peer_query_kernel(x_ref, mod_ref, g_ref, wh_ref, wl_ref, sk_ref, ht_ref, s_ref):
    h = _norm_mod(x_ref[0], g_ref[...], mod_ref[0, 4], mod_ref[0, 3])
    h_hi = h.astype(BF16)
    ht_ref[0] = h.T.astype(BF16)
    q = jnp.dot(h_hi, wh_ref[...], preferred_element_type=F32)
    if PEER_P_ROUTE == 3:
        h_lo = (h - h_hi.astype(F32)).astype(BF16)
        q = q + jnp.dot(h_lo, wh_ref[...], preferred_element_type=F32) \
            + jnp.dot(h_hi, wl_ref[...], preferred_element_type=F32)
    dq = sk_ref.shape[2]
    for hp in range(sk_ref.shape[0]):
        s_ref[0, hp] = _mm(sk_ref[hp], q[:, hp * dq:(hp + 1) * dq], PEER_P_ROUTE, nt=True)


def _peer_query(x, mod, g, qw_hi, qw_lo, sub_keys):
    B, T, D = x.shape
    tm = 256
    nq = qw_hi.shape[1]
    sk = sub_keys.reshape(2 * PEER_H, PEER_NKEYS, -1)
    return pl.pallas_call(
        _peer_query_kernel,
        grid=(B, T // tm),
        in_specs=[
            pl.BlockSpec((1, tm, D), lambda b, t: (b, t, 0)),
            pl.BlockSpec((1, 6, 1, D), lambda b, t: (b, 0, 0, 0)),
            pl.BlockSpec((1, D), lambda b, t: (0, 0)),
            pl.BlockSpec((D, nq), lambda b, t: (0, 0)),
            pl.BlockSpec((D, nq), lambda b, t: (0, 0)),
            pl.BlockSpec(sk.shape, lambda b, t: (0, 0, 0)),
        ],
        out_specs=[pl.BlockSpec((1, D, tm), lambda b, t: (b, 0, t)),
                   pl.BlockSpec((1, 2 * PEER_H, PEER_NKEYS, tm), lambda b, t: (b, 0, 0, t))],
        out_shape=[jax.ShapeDtypeStruct((B, D, T), BF16),
                   jax.ShapeDtypeStruct((B, 2 * PEER_H, PEER_NKEYS, T), F32)],
        compiler_params=_cparams("parallel", "parallel"),
        name="peer_query",
    )(x, mod, g.reshape(1, D), qw_hi, qw_lo, sk)


def _extract_top(s, n, on_pick):
    rows = lax.broadcasted_iota(jnp.int32, s.shape, 0).astype(F32)
    big = float(s.shape[0])
    for it in range(n):
        m = jnp.max(s, axis=0, keepdims=True)
        idx = jnp.min(jnp.where(s == m, rows, big), axis=0, keepdims=True)
        sel = rows == idx
        on_pick(it, m, sel)
        s = jnp.where(sel, -jnp.inf, s)


def _dup_bf16(x):
    bits = lax.bitcast_convert_type(x.astype(BF16).astype(F32), jnp.uint32)
    return bits | (bits >> 16)


def _peer_route_kernel(s_ref, rank2_ref, cnt_ref, e1_ref, e2_ref, v1_scr, v2_scr, cand_scr):
    tn = s_ref.shape[3]
    for h in range(PEER_H):
        s1 = s_ref[0, 2 * h]
        s2 = s_ref[0, 2 * h + 1]
        ranks = []
        for s, scr in ((s1, v1_scr), (s2, v2_scr)):
            rank = [jnp.full(s.shape, float(PEER_NKEYS), F32)]

            def pick(it, m, sel, scr=scr, rank=rank):
                scr[it:it + 1, :] = m
                rank[0] = jnp.where(sel, float(it), rank[0])

            _extract_top(s, PEER_TOPK, pick)
            ranks.append(rank[0])
        rank1, rank2 = ranks
        cand_scr[...] = jnp.full(cand_scr.shape, -jnp.inf, F32)
        for ci, (a, b) in enumerate(_CAND):
            cand_scr[ci:ci + 1, :] = v1_scr[a:a + 1, :] + v2_scr[b:b + 1, :]
        picked = [jnp.zeros(cand_scr.shape, F32), jnp.zeros((1, tn), F32), None]

        def pick_c(it, m, sel, picked=picked):
            if it == 0:
                picked[2] = m
            picked[0] = jnp.where(sel, 1.0, picked[0])
            picked[1] = picked[1] + jnp.exp(m - picked[2])

        _extract_top(cand_scr[...], PEER_TOPK, pick_c)
        selmask, z = picked[0], picked[1]
        ca = lax.broadcasted_iota(jnp.int32, (PEER_TOPK, _NCAND_PAD), 0)
        cj = lax.broadcasted_iota(jnp.int32, (PEER_TOPK, _NCAND_PAD), 1)
        row_of = functools.reduce(jnp.logical_or, [(ca == a) & (cj == ci) for ci, (a, _) in enumerate(_CAND)])
        cnt_a = jnp.dot(row_of.astype(BF16), selmask.astype(BF16), preferred_element_type=F32)
        cnt = jnp.zeros(s1.shape, F32)
        for a in range(PEER_TOPK):
            cnt = jnp.where(rank1 == float(a), cnt_a[a:a + 1, :], cnt)
        rank2_ref[0, h] = rank2.astype(BF16)
        cnt_ref[0, h] = _dup_bf16(cnt)
        e1_ref[0, h] = _dup_bf16(jnp.exp(s1 - v1_scr[0:1, :]) / z)
        e2_ref[0, h] = jnp.exp(s2 - v2_scr[0:1, :]).astype(BF16)


def _peer_route(sT):
    B, _, NK, T = sT.shape
    tn = 256
    ospec = pl.BlockSpec((1, PEER_H, NK, tn), lambda b, t: (b, 0, 0, t))
    oshape = lambda dt: jax.ShapeDtypeStruct((B, PEER_H, NK, T), dt)
    return pl.pallas_call(
        _peer_route_kernel,
        grid=(B, T // tn),
        in_specs=[pl.BlockSpec((1, 2 * PEER_H, NK, tn), lambda b, t: (b, 0, 0, t))],
        out_specs=[ospec] * 4,
        out_shape=[oshape(BF16), oshape(jnp.uint32), oshape(jnp.uint32), oshape(BF16)],
        scratch_shapes=[pltpu.VMEM((PEER_TOPK, tn), F32), pltpu.VMEM((PEER_TOPK, tn), F32),
                        pltpu.VMEM((_NCAND_PAD, tn), F32)],
        compiler_params=_cparams("parallel", "parallel"),
        name="peer_route",
    )(sT)


_GELU_C = float(np.sqrt(2.0 / np.pi))
PEER_SUB = 256


def _gelu_tanh(x):
    hx = 0.5 * x
    return hx + hx * jnp.tanh(x * (_GELU_C + (_GELU_C * 0.044715) * (x * x)))


def _row_tile(words):
    return pltpu.bitcast(jnp.broadcast_to(words, (PEER_NKEYS // 2, words.shape[1])), BF16)


def _peer_expert_kernel(ht_ref, rank2_ref, cnt_ref, e1_ref, e2_ref, u_ref, vt_ref, x_ref, mod_ref,
                        fg_ref, o_ref, acc_ref, g_scr, *, te, final_norm):
    e = pl.program_id(2)

    @pl.when(e == 0)
    def _():
        acc_ref[...] = jnp.zeros(acc_ref.shape, F32)

    ht = ht_ref[0]
    rows = te // PEER_NKEYS
    for ii in range(rows):
        i = e * rows + ii
        g = None
        for hd in range(PEER_H):
            cnt_row = _row_tile(cnt_ref[0, hd, pl.ds(i, 1), :])
            e1_row = _row_tile(e1_ref[0, hd, pl.ds(i, 1), :])
            gh = jnp.where(rank2_ref[0, hd] < cnt_row, e2_ref[0, hd] * e1_row, jnp.zeros((), BF16))
            g = gh if g is None else g + gh
        g_scr[ii * PEER_NKEYS:(ii + 1) * PEER_NKEYS, :] = g
    for sb in range(te // PEER_SUB):
        es = slice(sb * PEER_SUB, (sb + 1) * PEER_SUB)
        a = jnp.dot(u_ref[es, :], ht, preferred_element_type=F32)
        g_scr[es, :] = _gelu_tanh(a.astype(BF16)) * g_scr[es, :]
    acc_ref[...] += jnp.dot(vt_ref[...], g_scr[...], preferred_element_type=F32)

    @pl.when(e == pl.num_programs(2) - 1)
    def _():
        y = x_ref[0] + mod_ref[0, 5] * acc_ref[...].T
        if final_norm:
            y = y * lax.rsqrt(jnp.mean(y * y, axis=-1, keepdims=True) + NORM_EPS) * fg_ref[...]
        o_ref[0] = y


def _peer_expert(h2, route, u_bf, vt_bf, x, mod, final_g, final_norm):
    B, T, D = x.shape
    tn = min(512, T)
    te = 2048
    ne = u_bf.shape[0] // te
    rspec = pl.BlockSpec((1, PEER_H, PEER_NKEYS, tn), lambda b, t, e: (b, 0, 0, t))
    return pl.pallas_call(
        functools.partial(_peer_expert_kernel, te=te, final_norm=final_norm),
        grid=(B, T // tn, ne),
        in_specs=[
            pl.BlockSpec((1, D, tn), lambda b, t, e: (b, 0, t)),
            rspec, rspec, rspec, rspec,
            pl.BlockSpec((te, D), lambda b, t, e: (e, 0)),
            pl.BlockSpec((D, te), lambda b, t, e: (0, e)),
            pl.BlockSpec((1, tn, D), lambda b, t, e: (b, t, 0)),
            pl.BlockSpec((1, 6, 1, D), lambda b, t, e: (b, 0, 0, 0)),
            pl.BlockSpec((1, D), lambda b, t, e: (0, 0)),
        ],
        out_specs=pl.BlockSpec((1, tn, D), lambda b, t, e: (b, t, 0)),
        out_shape=jax.ShapeDtypeStruct((B, T, D), F32),
        scratch_shapes=[pltpu.VMEM((D, tn), F32), pltpu.VMEM((te, tn), BF16)],
        compiler_params=_cparams("parallel", "parallel", "arbitrary"),
        name="peer_expert",
    )(h2, *route, u_bf, vt_bf, x, mod, final_g.reshape(1, D))


def _peer(x, mod, g, qw_hi, qw_lo, sub_keys, u_bf, vt_bf, final_g, final_norm):
    h2, sT = _peer_query(x, mod, g, qw_hi, qw_lo, sub_keys)
    route = _peer_route(sT)
    return _peer_expert(h2, route, u_bf, vt_bf, x, mod, final_g, final_norm)


def _split_bf16(w):
    hi = w.astype(BF16)
    return hi, (w - hi.astype(F32)).astype(BF16)


def kernel(x, c, ctx, c_ctx, ada_w, ada_b, norm1_g, norm2_g, w_in, rw_w0, rw_w_up, rw_a0, rw_a_up, rw_g_up,
           rw_k_k, rw_k_a, rw_r_k, rw_lnx_g, na_rpb, sc_conv_w, w_out, peer_q_w, peer_sub_keys, peer_u,
           peer_v, final_g):
    B, S, D = x.shape
    depth = ada_w.shape[0]
    rows = S // GRID_W
    kr = min(NA_KR, rows)
    n_mod = -(-(B + 1) // 8) * 8
    cc = jnp.concatenate([c, c_ctx[None, :], jnp.zeros((n_mod - B - 1, D), F32)], axis=0)
    mod = _ada(cc, ada_w, ada_b)
    xl, xc = x, ctx
    for i in range(depth):
        need_ctx = i < depth - 1
        mod_l = mod[i, :B].reshape(B, 6, 1, D)
        mod_c = jnp.broadcast_to(mod[i, B].reshape(1, 6, 1, D), (B, 6, 1, D))
        w_in_bf = w_in[i].astype(BF16)
        w_out_bf = w_out[i].astype(BF16)
        rw = (rw_w0[i], rw_w_up[i], rw_a0[i], rw_a_up[i], rw_k_k[i], rw_k_a[i])

        zrw_l, q_l, k_l, v_l, zsc_l = _proj_in(xl, mod_l, norm1_g[i], w_in_bf)
        zrw_c, q_c, k_c, v_c, zsc_c = _proj_in(xc, mod_c, norm1_g[i], w_in_bf)
        y2_c, y2_l = _rwkv_state(_rwkv_prep(zrw_c, *rw), _rwkv_prep(zrw_l, *rw))
        yrw_l = _rwkv_out(y2_l, zrw_l, rw_g_up[i], rw_r_k[i], rw_lnx_g[i])
        yna_l = _na_attention(q_l, k_l, v_l, k_c, v_c, _na_bias_table(na_rpb[i], rows, kr))
        ysc_l = _short_conv(zsc_l, sc_conv_w[i])
        xl = _proj_out(yrw_l, yna_l, ysc_l, w_out_bf, xl, mod_l)

        qw_hi, qw_lo = _split_bf16(peer_q_w[i])
        u_bf = peer_u[i].astype(BF16)
        vt_bf = peer_v[i].astype(BF16).T
        peer_args = (norm2_g[i], qw_hi, qw_lo, peer_sub_keys[i], u_bf, vt_bf, final_g)
        xl = _peer(xl, mod_l, *peer_args, final_norm=not need_ctx)
        if need_ctx:
            yrw_c = _rwkv_out(y2_c, zrw_c, rw_g_up[i], rw_r_k[i], rw_lnx_g[i])
            yna_c = _ctx_attention(q_c, k_c, v_c)
            ysc_c = _short_conv(zsc_c, sc_conv_w[i])
            xc = _proj_out(yrw_c, yna_c, ysc_c, w_out_bf, xc, mod_c)
            xc = _peer(xc, mod_c, *peer_args, final_norm=False)
    return xl
```

```python
import functools

import numpy as np
import jax
import jax.numpy as jnp
from jax import lax
from jax.experimental import pallas as pl
from jax.experimental.pallas import tpu as pltpu

F32 = jnp.float32
BF16 = jnp.bfloat16

HEAD_DIM = 64
GRID_W = 64
RW_H = 4
RW_W = RW_H * HEAD_DIM
RW_LW = 64
RW_LA = 64
RW_LG = 128
RW_COLS = 3 * RW_W + 2 * RW_LW + 2 * RW_LA + RW_LG
NA_H = 8
NA_W = NA_H * HEAD_DIM
NA_KR = 8
NA_KC = 16
SC_W = 256
PEER_H = 8
PEER_NKEYS = 128
PEER_TOPK = 16
NORM_EPS = 1e-6
GN_EPS = 64e-5
NEG_INF = -1e30
RW_CHUNK = 64

VMEM_LIMIT = 52 * 1024 * 1024

_CAND = [(a, b) for a in range(PEER_TOPK) for b in range(PEER_TOPK) if (a + 1) * (b + 1) <= PEER_TOPK]
_NCAND = len(_CAND)
_NCAND_PAD = -(-_NCAND // 8) * 8


def _cparams(*sem):
    return pltpu.CompilerParams(dimension_semantics=sem, vmem_limit_bytes=VMEM_LIMIT)


def _mm(a, b, passes=1, nt=False):
    dims = (((1,), (1,)), ((), ())) if nt else (((1,), (0,)), ((), ()))
    dg = functools.partial(lax.dot_general, dimension_numbers=dims, preferred_element_type=F32)
    if passes == 6:
        return dg(a.astype(F32), b.astype(F32), precision=lax.Precision.HIGHEST)
    a_hi = a.astype(BF16)
    b_hi = b.astype(BF16)
    out = dg(a_hi, b_hi)
    if passes == 3:
        a_lo = (a - a_hi.astype(F32)).astype(BF16)
        b_lo = (b - b_hi.astype(F32)).astype(BF16)
        out = out + dg(a_lo, b_hi) + dg(a_hi, b_lo)
    return out


def _sigmoid(x):
    return 1.0 / (1.0 + jnp.exp(-x))


def _norm_mod(x, g, scale, shift):
    ms = jnp.mean(x * x, axis=-1, keepdims=True)
    y = x * lax.rsqrt(ms + NORM_EPS) * g
    return y * (1.0 + scale) + shift


def _ada_kernel(c_ref, w_ref, b_ref, o_ref):
    cc = c_ref[...]
    o_ref[0] = _mm(cc * _sigmoid(cc), w_ref[0], 6) + b_ref[0]


def _ada(cc, ada_w, ada_b):
    L, D, D6 = ada_w.shape
    R = cc.shape[0]
    tn = 1536
    return pl.pallas_call(
        _ada_kernel,
        grid=(L, D6 // tn),
        in_specs=[
            pl.BlockSpec((R, D), lambda l, j: (0, 0)),
            pl.BlockSpec((1, D, tn), lambda l, j: (l, 0, j)),
            pl.BlockSpec((1, 1, tn), lambda l, j: (l, 0, j)),
        ],
        out_specs=pl.BlockSpec((1, R, tn), lambda l, j: (l, 0, j)),
        out_shape=jax.ShapeDtypeStruct((L, R, D6), F32),
        compiler_params=_cparams("parallel", "parallel"),
        name="ada",
    )(cc, ada_w, ada_b.reshape(L, 1, D6))


def _proj_in_kernel(x_ref, mod_ref, g_ref, w_ref, zrw_ref, q_ref, k_ref, v_ref, zsc_ref):
    h = _norm_mod(x_ref[0], g_ref[...], mod_ref[0, 1], mod_ref[0, 0])
    z = jnp.dot(h.astype(BF16), w_ref[...], preferred_element_type=F32)
    o = RW_COLS
    zrw_ref[0] = z[:, :o]
    q_ref[0] = (z[:, o:o + NA_W] * (HEAD_DIM ** -0.5)).astype(BF16)
    k_ref[0] = z[:, o + NA_W:o + 2 * NA_W].astype(BF16)
    v_ref[0] = z[:, o + 2 * NA_W:o + 3 * NA_W].astype(BF16)
    zsc_ref[0] = z[:, o + 3 * NA_W:]


def _proj_in(x, mod, g, w_bf):
    B, T, D = x.shape
    tm = 256
    n_out = w_bf.shape[1]
    row = lambda b, t: (b, t, 0)
    return pl.pallas_call(
        _proj_in_kernel,
        grid=(B, T // tm),
        in_specs=[
            pl.BlockSpec((1, tm, D), row),
            pl.BlockSpec((1, 6, 1, D), lambda b, t: (b, 0, 0, 0)),
            pl.BlockSpec((1, D), lambda b, t: (0, 0)),
            pl.BlockSpec((D, n_out), lambda b, t: (0, 0)),
        ],
        out_specs=[
            pl.BlockSpec((1, tm, RW_COLS), row),
            pl.BlockSpec((1, tm, NA_W), row),
            pl.BlockSpec((1, tm, NA_W), row),
            pl.BlockSpec((1, tm, NA_W), row),
            pl.BlockSpec((1, tm, 3 * SC_W), row),
        ],
        out_shape=[
            jax.ShapeDtypeStruct((B, T, RW_COLS), F32),
            jax.ShapeDtypeStruct((B, T, NA_W), BF16),
            jax.ShapeDtypeStruct((B, T, NA_W), BF16),
            jax.ShapeDtypeStruct((B, T, NA_W), BF16),
            jax.ShapeDtypeStruct((B, T, 3 * SC_W), F32),
        ],
        compiler_params=_cparams("parallel", "parallel"),
        name="proj_in",
    )(x, mod, g.reshape(1, D), w_bf)


RW_CHUNKS_PER_STEP = 2
RW_P_ALG = 1
RW_P_LORA = 3


def _rwkv_prep_kernel(z_ref, w0_ref, wup_ref, a0_ref, aup_ref, kk_ref, ka_ref,
                      phi_ref, zc_ref, wr_ref, yin_ref):
    C = RW_CHUNK
    W = RW_W
    assert C == HEAD_DIM
    N = RW_H * C
    row = lax.broadcasted_iota(jnp.int32, (C, C), 0)
    col = lax.broadcasted_iota(jnp.int32, (C, C), 1)
    brow = lax.broadcasted_iota(jnp.int32, (N, W), 0)
    bcol = lax.broadcasted_iota(jnp.int32, (N, W), 1)
    bd = (brow // C) == (bcol // HEAD_DIM)
    eye_j = lax.broadcasted_iota(jnp.int32, (HEAD_DIM, W), 0) == \
        lax.broadcasted_iota(jnp.int32, (HEAD_DIM, W), 1) % HEAD_DIM
    head_ones = bd.astype(F32)
    mm = functools.partial(_mm, passes=RW_P_ALG)
    mm_tn = lambda a, b: lax.dot_general(a.astype(BF16), b.astype(BF16), (((0,), (0,)), ((), ())),
                                         preferred_element_type=F32)
    collapse = lambda m: functools.reduce(lambda a, b: a + b, [m[h * C:(h + 1) * C] for h in range(RW_H)])

    def direction(ci, d):
        z = z_ref[0, ci * C:(ci + 1) * C, :]
        r = z[:, 0:W]
        k = z[:, W:2 * W]
        v = z[:, 2 * W:3 * W]
        zw = z[:, 3 * W:3 * W + 2 * RW_LW]
        za = z[:, 3 * W + 2 * RW_LW:3 * W + 2 * RW_LW + 2 * RW_LA]
        kk = k * kk_ref[...]
        kk = kk * lax.rsqrt(_mm(kk * kk, head_ones, 3) + 1e-12)
        lw = jnp.tanh(zw[:, d * RW_LW:(d + 1) * RW_LW])
        la = za[:, d * RW_LA:(d + 1) * RW_LA]
        w_raw = w0_ref[d:d + 1, :] + _mm(lw, wup_ref[d], RW_P_LORA)
        softplus = jnp.maximum(-w_raw, 0.0) + jnp.log(1.0 + jnp.exp(-jnp.abs(w_raw)))
        logw = -jnp.exp(-softplus - 0.5)
        a = _sigmoid(a0_ref[d:d + 1, :] + _mm(la, aup_ref[d], RW_P_LORA))
        k_d = k * (1.0 + (a - 1.0) * ka_ref[...])
        b_d = kk * a
        if d == 0:
            tri_incl, tri_strict = col <= row, col < row
        else:
            tri_incl, tri_strict = col >= row, col > row
        cum = _mm(tri_incl.astype(F32), logw, 6)
        tot = jnp.sum(logw, axis=0, keepdims=True)
        e_neg = jnp.exp(-cum)
        e_end = jnp.exp(tot - cum)
        a_t = -kk * jnp.exp(cum - logw)
        r_t = r * jnp.exp(cum)
        b_t = b_d * e_neg
        k_t = k_d * e_neg
        b_p = b_d * e_end
        k_p = k_d * e_end
        p_c = jnp.exp(tot)
        bdx = lambda x: jnp.where(bd, jnp.concatenate([x] * RW_H, axis=0), 0.0)
        a_bd, r_bd, v_bd = bdx(a_t), bdx(r_t), bdx(v)
        yield
        p4 = mm(jnp.concatenate([a_bd, r_bd], axis=0),
                jnp.concatenate([bdx(b_t), bdx(k_t)], axis=0), nt=True)
        yield
        tri_strict_bd = bd & ((bcol < brow) if d == 0 else (bcol > brow))
        tri_incl_bd = bd & ((bcol <= brow) if d == 0 else (bcol >= brow))
        l_ab = jnp.where(tri_strict_bd, p4[:N, :N], 0.0)
        l_ak = jnp.where(tri_strict_bd, p4[:N, N:], 0.0)
        m_rb = jnp.where(tri_incl_bd, p4[N:, :N], 0.0)
        m_rk = jnp.where(tri_incl_bd, p4[N:, N:], 0.0)
        t_inv = jnp.where(brow == bcol, 1.0, l_ab)
        pw = l_ab
        n = 2
        lakv = mm(l_ak, v_bd)
        mrkv = mm(m_rk, v_bd)
        kpv = mm_tn(bdx(k_p), v_bd)
        while n < C:
            pw = mm(pw, pw)
            yield
            t_inv = t_inv + mm(t_inv, pw)
            n *= 2
        yield
        x = mm(t_inv, jnp.concatenate([a_bd, lakv], axis=1))
        yield
        wy = mm(m_rb, x)
        pz = mm_tn(bdx(b_p), x)
        wr_ref[0, ci, d] = r_t + collapse(wy[:, :W])
        yin_ref[0, ci, d] = collapse(wy[:, W:] + mrkv)
        phi_ref[0, ci, d] = jnp.where(eye_j, p_c, 0.0) + collapse(pz[:, :W])
        zc_ref[0, ci, d] = collapse(pz[:, W:] + kpv)

    live = [direction(ci, d) for ci in range(z_ref.shape[1] // C) for d in range(2)]
    done = object()
    while live:
        live = [g for g in live if next(g, done) is not done]


def _rwkv_prep(zrw, w0, w_up, a0, a_up, k_k, k_a):
    B, T, _ = zrw.shape
    C = RW_CHUNK
    nch = T // C
    full = lambda a: pl.BlockSpec(a.shape, lambda b, c: (0,) * a.ndim)
    k_k = k_k.reshape(1, RW_W)
    k_a = k_a.reshape(1, RW_W)
    cps = RW_CHUNKS_PER_STEP
    ospec = pl.BlockSpec((1, cps, 2, HEAD_DIM, RW_W), lambda b, c: (b, c, 0, 0, 0))
    oshape = jax.ShapeDtypeStruct((B, nch, 2, HEAD_DIM, RW_W), F32)
    return pl.pallas_call(
        _rwkv_prep_kernel,
        grid=(B, nch // cps),
        in_specs=[pl.BlockSpec((1, cps * C, RW_COLS), lambda b, c: (b, c, 0)),
                  full(w0), full(w_up), full(a0), full(a_up), full(k_k), full(k_a)],
        out_specs=[ospec] * 4,
        out_shape=[oshape] * 4,
        compiler_params=_cparams("parallel", "parallel"),
        name="rwkv_prep",
    )(zrw, w0, w_up, a0, a_up, k_k, k_a)


RW_P_STATE = 3


def _rwkv_state_kernel(phic, zcc, wrc, yinc, phil, zcl, wrl, yinl, yc_ref, yl_ref):
    d = pl.program_id(1)
    mm = functools.partial(_mm, passes=RW_P_STATE)

    N = RW_H * HEAD_DIM
    bd = (lax.broadcasted_iota(jnp.int32, (N, RW_W), 0) // HEAD_DIM) == \
        (lax.broadcasted_iota(jnp.int32, (N, RW_W), 1) // HEAD_DIM)

    def run(H, phi, zc, wr, yin, y_ref):
        nch = phi.shape[1]

        def body(s, H):
            c = jnp.where(d == 0, s, nch - 1 - s)
            t0 = pl.multiple_of(c * RW_CHUNK, RW_CHUNK)
            h_bd = jnp.where(bd, jnp.concatenate([H] * RW_H, axis=0), 0.0)
            ops = jnp.concatenate([wr[0, c, 0], phi[0, c, 0]], axis=0)
            out = mm(ops, h_bd) + jnp.concatenate([yin[0, c, 0], zc[0, c, 0]], axis=0)
            y_ref[0, 0, pl.ds(t0, RW_CHUNK), :] = out[:RW_CHUNK]
            return out[RW_CHUNK:]

        return lax.fori_loop(0, nch, body, H)

    H = jnp.zeros((HEAD_DIM, RW_W), F32)
    H = run(H, phic, zcc, wrc, yinc, yc_ref)
    run(H, phil, zcl, wrl, yinl, yl_ref)


def _rwkv_state(ops_c, ops_l):
    B, nc = ops_c[0].shape[:2]
    nl = ops_l[0].shape[1]
    spec = lambda n: pl.BlockSpec((1, n, 1, HEAD_DIM, RW_W), lambda b, d: (b, 0, d, 0, 0))
    yspec = lambda n: pl.BlockSpec((1, 1, n * RW_CHUNK, RW_W), lambda b, d: (b, d, 0, 0))
    return pl.pallas_call(
        _rwkv_state_kernel,
        grid=(B, 2),
        in_specs=[spec(nc)] * 4 + [spec(nl)] * 4,
        out_specs=[yspec(nc), yspec(nl)],
        out_shape=[jax.ShapeDtypeStruct((B, 2, nc * RW_CHUNK, RW_W), F32),
                   jax.ShapeDtypeStruct((B, 2, nl * RW_CHUNK, RW_W), F32)],
        compiler_params=_cparams("parallel", "parallel"),
        name="rwkv_state",
    )(*ops_c, *ops_l)


def _rwkv_out_kernel(yf_ref, yr_ref, z_ref, gup_ref, rk_ref, lnx_ref, o_ref):
    W = RW_W
    y = yf_ref[0, 0] + yr_ref[0, 0]
    z = z_ref[0]
    r, k, v = z[:, 0:W], z[:, W:2 * W], z[:, 2 * W:3 * W]
    zg = z[:, RW_COLS - RW_LG:RW_COLS]
    hrow = lax.broadcasted_iota(jnp.int32, (W, W), 0) // HEAD_DIM
    hcol = lax.broadcasted_iota(jnp.int32, (W, W), 1) // HEAD_DIM
    head_ones = (hrow == hcol).astype(F32)
    inv = 1.0 / HEAD_DIM
    mu = _mm(y, head_ones, 3) * inv
    yc = y - mu
    var = _mm(yc * yc, head_ones, 3) * inv
    yn = yc * lax.rsqrt(var + GN_EPS) * lnx_ref[...]
    bonus = _mm(r * k * rk_ref[...], head_ones, 3)
    yn = yn + bonus * v
    gate = _mm(_sigmoid(zg), gup_ref[...], 3)
    o_ref[0] = yn * gate


def _rwkv_out(y2, zrw, g_up, r_k, lnx_g):
    B, _, T, W = y2.shape
    tm = 256
    return pl.pallas_call(
        _rwkv_out_kernel,
        grid=(B, T // tm),
        in_specs=[
            pl.BlockSpec((1, 1, tm, W), lambda b, t: (b, 0, t, 0)),
            pl.BlockSpec((1, 1, tm, W), lambda b, t: (b, 1, t, 0)),
            pl.BlockSpec((1, tm, RW_COLS), lambda b, t: (b, t, 0)),
            pl.BlockSpec((RW_LG, W), lambda b, t: (0, 0)),
            pl.BlockSpec((1, W), lambda b, t: (0, 0)),
            pl.BlockSpec((1, W), lambda b, t: (0, 0)),
        ],
        out_specs=pl.BlockSpec((1, tm, W), lambda b, t: (b, t, 0)),
        out_shape=jax.ShapeDtypeStruct((B, T, W), F32),
        compiler_params=_cparams("parallel", "parallel"),
        name="rwkv_out",
    )(y2, y2, zrw, g_up, r_k.reshape(1, W), lnx_g.reshape(1, W))


def _softmax_av_steps(scores, values, store):
    m = functools.reduce(jnp.maximum, [jnp.max(s, axis=-1, keepdims=True) for s in scores])
    yield
    ps = [jnp.exp(s - m) for s in scores]
    l = functools.reduce(lambda a, b: a + b, [jnp.sum(p, axis=-1, keepdims=True) for p in ps])
    yield
    o = functools.reduce(lambda a, b: a + b,
                         [jnp.dot(p.astype(BF16), v, preferred_element_type=F32) for p, v in zip(ps, values)])
    yield
    store(o / l)


def _interleave(chains):
    live = list(chains)
    done = object()
    while live:
        live = [g for g in live if next(g, done) is not done]


NA_HEADS_INTERLEAVED = 8


def _na_kernel(q_ref, k_ref, v_ref, kc_ref, vc_ref, bias_ref, o_ref, *, kr):
    r = pl.program_id(1)
    rows = pl.num_programs(1)
    start = pl.multiple_of(jnp.clip(r - NA_KR // 2, 0, rows - kr) * GRID_W, GRID_W)
    nwin = kr * GRID_W

    def head(h):
        sl = slice(h * HEAD_DIM, (h + 1) * HEAD_DIM)
        q = q_ref[0, :, sl]
        kb = k_ref[0, pl.ds(start, nwin), sl]
        vb = v_ref[0, pl.ds(start, nwin), sl]
        s_win = _mm(q, kb, nt=True) + bias_ref[0, h]
        s_ctx = _mm(q, kc_ref[0, :, sl], nt=True)
        yield

        def store(o):
            o_ref[0, :, sl] = o

        yield from _softmax_av_steps([s_win, s_ctx], [vb, vc_ref[0, :, sl]], store)

    for h0 in range(0, NA_H, NA_HEADS_INTERLEAVED):
        _interleave(head(h) for h in range(h0, h0 + NA_HEADS_INTERLEAVED))


def _na_bias_table(rpb, rows, kr):
    ci = np.arange(GRID_W)
    c0 = np.clip(ci - NA_KC // 2, 0, GRID_W - NA_KC)
    col_ok = (ci[None, :] >= c0[:, None]) & (ci[None, :] < c0[:, None] + NA_KC)
    dc = np.clip(ci[None, :] - ci[:, None] + (NA_KC - 1), 0, 2 * NA_KC - 2)
    n_off = NA_KR
    dr = np.arange(kr)[None, :] - np.arange(n_off)[:, None] + (NA_KR - 1)
    dr_ok = (dr >= 0) & (dr < 2 * NA_KR - 1)
    sel_r = (dr[:, :, None] == np.arange(2 * NA_KR - 1)).astype(np.float32)
    sel_c = (dc[:, :, None] == np.arange(2 * NA_KC - 1)).astype(np.float32)
    bias = jnp.einsum('hrc,okr,qwc->ohqkw', rpb.astype(F32), sel_r, sel_c, precision=lax.Precision.HIGHEST)
    ok = dr_ok[:, None, None, :, None] & col_ok[None, None, :, None, :]
    bias = jnp.where(ok, bias, NEG_INF)
    return bias.reshape(n_off, NA_H, GRID_W, kr * GRID_W)


def _na_attention(q, k, v, kc, vc, bias_tab):
    B, S, W = q.shape
    Cn = kc.shape[1]
    rows = S // GRID_W
    kr = min(NA_KR, rows)
    nwin = kr * GRID_W

    def bias_idx(b, r):
        return (r - jnp.clip(r - NA_KR // 2, 0, rows - kr), 0, 0, 0)

    return pl.pallas_call(
        functools.partial(_na_kernel, kr=kr),
        grid=(B, rows),
        in_specs=[
            pl.BlockSpec((1, GRID_W, W), lambda b, r: (b, r, 0)),
            pl.BlockSpec((1, S, W), lambda b, r: (b, 0, 0)),
            pl.BlockSpec((1, S, W), lambda b, r: (b, 0, 0)),
            pl.BlockSpec((1, Cn, W), lambda b, r: (b, 0, 0)),
            pl.BlockSpec((1, Cn, W), lambda b, r: (b, 0, 0)),
            pl.BlockSpec((1, NA_H, GRID_W, nwin), bias_idx),
        ],
        out_specs=pl.BlockSpec((1, GRID_W, W), lambda b, r: (b, r, 0)),
        out_shape=jax.ShapeDtypeStruct((B, S, W), F32),
        compiler_params=_cparams("parallel", "arbitrary"),
        name="na_attn",
    )(q, k, v, kc, vc, bias_tab)


def _ctx_attn_kernel(q_ref, k_ref, v_ref, o_ref):
    def head(h):
        sl = slice(h * HEAD_DIM, (h + 1) * HEAD_DIM)
        s = _mm(q_ref[0, :, sl], k_ref[0, :, sl], nt=True)
        yield

        def store(o):
            o_ref[0, :, sl] = o

        yield from _softmax_av_steps([s], [v_ref[0, :, sl]], store)

    for h0 in range(0, NA_H, NA_HEADS_INTERLEAVED):
        _interleave(head(h) for h in range(h0, h0 + NA_HEADS_INTERLEAVED))


def _ctx_attention(q, k, v):
    B, Cn, W = q.shape
    spec = pl.BlockSpec((1, Cn, W), lambda b: (b, 0, 0))
    return pl.pallas_call(
        _ctx_attn_kernel,
        grid=(B,),
        in_specs=[spec] * 3,
        out_specs=spec,
        out_shape=jax.ShapeDtypeStruct((B, Cn, W), F32),
        compiler_params=_cparams("parallel"),
        name="ctx_attn",
    )(q, k, v)


def _conv_kernel(z_ref, w_ref, o_ref):
    T = z_ref.shape[1]
    z = z_ref[0]
    zb, zc, zx = z[:, :SC_W], z[:, SC_W:2 * SC_W], z[:, 2 * SC_W:]
    u = zc * zx
    t = lax.broadcasted_iota(jnp.int32, u.shape, 0)
    prev = jnp.where(t == 0, 0.0, pltpu.roll(u, 1, 0))
    nxt = jnp.where(t == T - 1, 0.0, pltpu.roll(u, T - 1, 0))
    o_ref[0] = zb * (prev * w_ref[0:1, :] + u * w_ref[1:2, :] + nxt * w_ref[2:3, :])


def _short_conv(zsc, w):
    B, T, _ = zsc.shape
    return pl.pallas_call(
        _conv_kernel,
        grid=(B,),
        in_specs=[pl.BlockSpec((1, T, 3 * SC_W), lambda b: (b, 0, 0)),
                  pl.BlockSpec((3, SC_W), lambda b: (0, 0))],
        out_specs=pl.BlockSpec((1, T, SC_W), lambda b: (b, 0, 0)),
        out_shape=jax.ShapeDtypeStruct((B, T, SC_W), F32),
        compiler_params=_cparams("parallel"),
        name="short_conv",
    )(zsc, w)


def _proj_out_kernel(yrw_ref, yna_ref, ysc_ref, w_ref, x_ref, mod_ref, o_ref):
    a, b = RW_W, RW_W + NA_W
    y = (jnp.dot(yrw_ref[0].astype(BF16), w_ref[:a, :], preferred_element_type=F32)
         + jnp.dot(yna_ref[0].astype(BF16), w_ref[a:b, :], preferred_element_type=F32)
         + jnp.dot(ysc_ref[0].astype(BF16), w_ref[b:, :], preferred_element_type=F32))
    o_ref[0] = x_ref[0] + mod_ref[0, 2] * y


def _proj_out(yrw, yna, ysc, w_bf, x, mod):
    B, T, D = x.shape
    tm = 256
    row = lambda b, t: (b, t, 0)
    return pl.pallas_call(
        _proj_out_kernel,
        grid=(B, T // tm),
        in_specs=[
            pl.BlockSpec((1, tm, RW_W), row),
            pl.BlockSpec((1, tm, NA_W), row),
            pl.BlockSpec((1, tm, SC_W), row),
            pl.BlockSpec(w_bf.shape, lambda b, t: (0, 0)),
            pl.BlockSpec((1, tm, D), row),
            pl.BlockSpec((1, 6, 1, D), lambda b, t: (b, 0, 0, 0)),
        ],
        out_specs=pl.BlockSpec((1, tm, D), row),
        out_shape=jax.ShapeDtypeStruct((B, T, D), F32),
        compiler_params=_cparams("parallel", "parallel"),
        name="proj_out",
    )(yrw, yna, ysc, w_bf, x, mod)


PEER_P_ROUTE = 3


def _peer_query_kernel(x_ref, mod_ref, g_ref, wh_ref, wl_ref, sk_ref, ht_ref, s_ref):
    h = _norm_mod(x_ref[0], g_ref[...], mod_ref[0, 4], mod_ref[0, 3])
    h_hi = h.astype(BF16)
    ht_ref[0] = h.T.astype(BF16)
    q = jnp.dot(h_hi, wh_ref[...], preferred_element_type=F32)
    if PEER_P_ROUTE == 3:
        h_lo = (h - h_hi.astype(F32)).astype(BF16)
        q = q + jnp.dot(h_lo, wh_ref[...], preferred_element_type=F32) \
            + jnp.dot(h_hi, wl_ref[...], preferred_element_type=F32)
    dq = sk_ref.shape[2]
    for hp in range(sk_ref.shape[0]):
        s_ref[0, hp] = _mm(sk_ref[hp], q[:, hp * dq:(hp + 1) * dq], PEER_P_ROUTE, nt=True)


def _peer_query(x, mod, g, qw_hi, qw_lo, sub_keys):
    B, T, D = x.shape
    tm = 256
    nq = qw_hi.shape[1]
    sk = sub_keys.reshape(2 * PEER_H, PEER_NKEYS, -1)
    return pl.pallas_call(
        _peer_query_kernel,
        grid=(B, T // tm),
        in_specs=[
            pl.BlockSpec((1, tm, D), lambda b, t: (b, t, 0)),
            pl.BlockSpec((1, 6, 1, D), lambda b, t: (b, 0, 0, 0)),
            pl.BlockSpec((1, D), lambda b, t: (0, 0)),
            pl.BlockSpec((D, nq), lambda b, t: (0, 0)),
            pl.BlockSpec((D, nq), lambda b, t: (0, 0)),
            pl.BlockSpec(sk.shape, lambda b, t: (0, 0, 0)),
        ],
        out_specs=[pl.BlockSpec((1, D, tm), lambda b, t: (b, 0, t)),
                   pl.BlockSpec((1, 2 * PEER_H, PEER_NKEYS, tm), lambda b, t: (b, 0, 0, t))],
        out_shape=[jax.ShapeDtypeStruct((B, D, T), BF16),
                   jax.ShapeDtypeStruct((B, 2 * PEER_H, PEER_NKEYS, T), F32)],
        compiler_params=_cparams("parallel", "parallel"),
        name="peer_query",
    )(x, mod, g.reshape(1, D), qw_hi, qw_lo, sk)


def _extract_top(s, n, on_pick):
    rows = lax.broadcasted_iota(jnp.int32, s.shape, 0).astype(F32)
    big = float(s.shape[0])
    for it in range(n):
        m = jnp.max(s, axis=0, keepdims=True)
        idx = jnp.min(jnp.where(s == m, rows, big), axis=0, keepdims=True)
        sel = rows == idx
        on_pick(it, m, sel)
        s = jnp.where(sel, -jnp.inf, s)


def _dup_bf16(x):
    bits = lax.bitcast_convert_type(x.astype(BF16).astype(F32), jnp.uint32)
    return bits | (bits >> 16)


def _batcher_pairs(lo, hi):
    def merge(lo, hi, r):
        step = r * 2
        if step < hi - lo:
            yield from merge(lo, hi, step)
            yield from merge(lo + r, hi, step)
            yield from [(i, i + r) for i in range(lo + r, hi - r, step)]
        else:
            yield (lo, lo + r)

    if hi - lo >= 1:
        mid = lo + (hi - lo) // 2
        yield from _batcher_pairs(lo, mid)
        yield from _batcher_pairs(mid + 1, hi)
        yield from merge(lo, hi, 1)


def _sorted_top(s):
    assert s.shape[0] == 8 * PEER_TOPK

    def cmpx(v, i, j):
        v[i], v[j] = jnp.maximum(v[i], v[j]), jnp.minimum(v[i], v[j])

    v = [s[8 * g:8 * g + 8] for g in range(PEER_TOPK)]
    for i, j in _batcher_pairs(0, PEER_TOPK - 1):
        cmpx(v, i, j)
    for shift in (4, 2, 1):
        v = [jnp.maximum(v[i], pltpu.roll(v[PEER_TOPK - 1 - i], shift, 0)) for i in range(PEER_TOPK)]
        stride = PEER_TOPK // 2
        while stride:
            for i in range(PEER_TOPK):
                if not i & stride:
                    cmpx(v, i, i + stride)
            stride //= 2
    return v


def _top_ranks(s, rank_scr, val_scr):
    nk = s.shape[0]
    v = _sorted_top(s)
    groups = [s[8 * g:8 * g + 8] for g in range(nk // 8)]
    ranks = []
    n_ge = None
    for grp in groups:
        r = jnp.full(grp.shape, float(nk), F32)
        for a in reversed(range(PEER_TOPK)):
            r = jnp.where(grp >= v[a], float(a), r)
        ranks.append(r)
        ge = jnp.where(grp >= v[PEER_TOPK - 1], 1.0, 0.0)
        n_ge = ge if n_ge is None else n_ge + ge
    rank_scr[...] = jnp.concatenate(ranks, axis=0)
    for a in range(PEER_TOPK):
        val_scr[a:a + 1, :] = v[a][0:1, :]
    gap = functools.reduce(jnp.minimum, [v[a] - v[a + 1] for a in range(PEER_TOPK - 1)])
    tied = jnp.where(gap[0:1, :] > 0.0, 0.0, 1.0) \
        + jnp.where(jnp.sum(n_ge, axis=0, keepdims=True) > float(PEER_TOPK), 1.0, 0.0)

    @pl.when(jnp.max(tied) > 0.0)
    def _():
        rank = [jnp.full(s.shape, float(nk), F32)]

        def pick(it, m, sel):
            val_scr[it:it + 1, :] = m
            rank[0] = jnp.where(sel, float(it), rank[0])

        _extract_top(s, PEER_TOPK, pick)
        rank_scr[...] = rank[0]


def _peer_route_kernel(s_ref, rank2_ref, cnt_ref, e1_ref, e2_ref, v1_scr, v2_scr, cand_scr, r1_scr, r2_scr):
    tn = s_ref.shape[3]
    for h in range(PEER_H):
        s1 = s_ref[0, 2 * h]
        s2 = s_ref[0, 2 * h + 1]
        _top_ranks(s1, r1_scr, v1_scr)
        _top_ranks(s2, r2_scr, v2_scr)
        rank1, rank2 = r1_scr[...], r2_scr[...]
        cand_scr[...] = jnp.full(cand_scr.shape, -jnp.inf, F32)
        for ci, (a, b) in enumerate(_CAND):
            cand_scr[ci:ci + 1, :] = v1_scr[a:a + 1, :] + v2_scr[b:b + 1, :]
        picked = [jnp.zeros(cand_scr.shape, F32), jnp.zeros((1, tn), F32), None]

        def pick_c(it, m, sel, picked=picked):
            if it == 0:
                picked[2] = m
            picked[0] = jnp.where(sel, 1.0, picked[0])
            picked[1] = picked[1] + jnp.exp(m - picked[2])

        _extract_top(cand_scr[...], PEER_TOPK, pick_c)
        selmask, z = picked[0], picked[1]
        ca = lax.broadcasted_iota(jnp.int32, (PEER_TOPK, _NCAND_PAD), 0)
        cj = lax.broadcasted_iota(jnp.int32, (PEER_TOPK, _NCAND_PAD), 1)
        row_of = functools.reduce(jnp.logical_or, [(ca == a) & (cj == ci) for ci, (a, _) in enumerate(_CAND)])
        cnt_a = jnp.dot(row_of.astype(BF16), selmask.astype(BF16), preferred_element_type=F32)
        cnt = jnp.zeros(s1.shape, F32)
        for a in range(PEER_TOPK):
            cnt = jnp.where(rank1 == float(a), cnt_a[a:a + 1, :], cnt)
        rank2_ref[0, h] = rank2.astype(BF16)
        cnt_ref[0, h] = _dup_bf16(cnt)
        e1_ref[0, h] = _dup_bf16(jnp.exp(s1 - v1_scr[0:1, :]) / z)
        e2_ref[0, h] = jnp.exp(s2 - v2_scr[0:1, :]).astype(BF16)


def _peer_route(sT):
    B, _, NK, T = sT.shape
    tn = min(512, T)
    ospec = pl.BlockSpec((1, PEER_H, NK, tn), lambda b, t: (b, 0, 0, t))
    oshape = lambda dt: jax.ShapeDtypeStruct((B, PEER_H, NK, T), dt)
    return pl.pallas_call(
        _peer_route_kernel,
        grid=(B, T // tn),
        in_specs=[pl.BlockSpec((1, 2 * PEER_H, NK, tn), lambda b, t: (b, 0, 0, t))],
        out_specs=[ospec] * 4,
        out_shape=[oshape(BF16), oshape(jnp.uint32), oshape(jnp.uint32), oshape(BF16)],
        scratch_shapes=[pltpu.VMEM((PEER_TOPK, tn), F32), pltpu.VMEM((PEER_TOPK, tn), F32),
                        pltpu.VMEM((_NCAND_PAD, tn), F32),
                        pltpu.VMEM((NK, tn), F32), pltpu.VMEM((NK, tn), F32)],
        compiler_params=_cparams("parallel", "parallel"),
        name="peer_route",
    )(sT)


_GELU_C = float(np.sqrt(2.0 / np.pi))
PEER_SUB = 256


def _gelu_tanh(x):
    hx = 0.5 * x
    return hx + hx * jnp.tanh(x * (_GELU_C + (_GELU_C * 0.044715) * (x * x)))


def _row_tile(words):
    return pltpu.bitcast(jnp.broadcast_to(words, (PEER_NKEYS // 2, words.shape[1])), BF16)


def _peer_expert_kernel(ht_ref, rank2_ref, cnt_ref, e1_ref, e2_ref, u_ref, vt_ref, x_ref, mod_ref,
                        fg_ref, o_ref, acc_ref, g_scr, *, te, final_norm):
    e = pl.program_id(2)

    @pl.when(e == 0)
    def _():
        acc_ref[...] = jnp.zeros(acc_ref.shape, F32)

    ht = ht_ref[0]
    rows = te // PEER_NKEYS
    for ii in range(rows):
        i = e * rows + ii
        g = None
        for hd in range(PEER_H):
            cnt_row = _row_tile(cnt_ref[0, hd, pl.ds(i, 1), :])
            e1_row = _row_tile(e1_ref[0, hd, pl.ds(i, 1), :])
            gh = jnp.where(rank2_ref[0, hd] < cnt_row, e2_ref[0, hd] * e1_row, jnp.zeros((), BF16))
            g = gh if g is None else g + gh
        g_scr[ii * PEER_NKEYS:(ii + 1) * PEER_NKEYS, :] = g
    for sb in range(te // PEER_SUB):
        es = slice(sb * PEER_SUB, (sb + 1) * PEER_SUB)
        a = jnp.dot(u_ref[es, :], ht, preferred_element_type=F32)
        g_scr[es, :] = _gelu_tanh(a.astype(BF16)) * g_scr[es, :]
    acc_ref[...] += jnp.dot(vt_ref[...], g_scr[...], preferred_element_type=F32)

    @pl.when(e == pl.num_programs(2) - 1)
    def _():
        y = x_ref[0] + mod_ref[0, 5] * acc_ref[...].T
        if final_norm:
            y = y * lax.rsqrt(jnp.mean(y * y, axis=-1, keepdims=True) + NORM_EPS) * fg_ref[...]
        o_ref[0] = y


def _peer_expert(h2, route, u_bf, vt_bf, x, mod, final_g, final_norm):
    B, T, D = x.shape
    tn = min(512, T)
    te = 2048
    ne = u_bf.shape[0] // te
    rspec = pl.BlockSpec((1, PEER_H, PEER_NKEYS, tn), lambda b, t, e: (b, 0, 0, t))
    return pl.pallas_call(
        functools.partial(_peer_expert_kernel, te=te, final_norm=final_norm),
        grid=(B, T // tn, ne),
        in_specs=[
            pl.BlockSpec((1, D, tn), lambda b, t, e: (b, 0, t)),
            rspec, rspec, rspec, rspec,
            pl.BlockSpec((te, D), lambda b, t, e: (e, 0)),
            pl.BlockSpec((D, te), lambda b, t, e: (0, e)),
            pl.BlockSpec((1, tn, D), lambda b, t, e: (b, t, 0)),
            pl.BlockSpec((1, 6, 1, D), lambda b, t, e: (b, 0, 0, 0)),
            pl.BlockSpec((1, D), lambda b, t, e: (0, 0)),
        ],
        out_specs=pl.BlockSpec((1, tn, D), lambda b, t, e: (b, t, 0)),
        out_shape=jax.ShapeDtypeStruct((B, T, D), F32),
        scratch_shapes=[pltpu.VMEM((D, tn), F32), pltpu.VMEM((te, tn), BF16)],
        compiler_params=_cparams("parallel", "parallel", "arbitrary"),
        name="peer_expert",
    )(h2, *route, u_bf, vt_bf, x, mod, final_g.reshape(1, D))


def _peer(x, mod, g, qw_hi, qw_lo, sub_keys, u_bf, vt_bf, final_g, final_norm):
    h2, sT = _peer_query(x, mod, g, qw_hi, qw_lo, sub_keys)
    route = _peer_route(sT)
    return _peer_expert(h2, route, u_bf, vt_bf, x, mod, final_g, final_norm)


def _split_bf16(w):
    hi = w.astype(BF16)
    return hi, (w - hi.astype(F32)).astype(BF16)


def kernel(x, c, ctx, c_ctx, ada_w, ada_b, norm1_g, norm2_g, w_in, rw_w0, rw_w_up, rw_a0, rw_a_up, rw_g_up,
           rw_k_k, rw_k_a, rw_r_k, rw_lnx_g, na_rpb, sc_conv_w, w_out, peer_q_w, peer_sub_keys, peer_u,
           peer_v, final_g):
    B, S, D = x.shape
    depth = ada_w.shape[0]
    rows = S // GRID_W
    kr = min(NA_KR, rows)
    n_mod = -(-(B + 1) // 8) * 8
    cc = jnp.concatenate([c, c_ctx[None, :], jnp.zeros((n_mod - B - 1, D), F32)], axis=0)
    mod = _ada(cc, ada_w, ada_b)
    xl, xc = x, ctx
    for i in range(depth):
        need_ctx = i < depth - 1
        mod_l = mod[i, :B].reshape(B, 6, 1, D)
        mod_c = jnp.broadcast_to(mod[i, B].reshape(1, 6, 1, D), (B, 6, 1, D))
        w_in_bf = w_in[i].astype(BF16)
        w_out_bf = w_out[i].astype(BF16)
        rw = (rw_w0[i], rw_w_up[i], rw_a0[i], rw_a_up[i], rw_k_k[i], rw_k_a[i])

        zrw_l, q_l, k_l, v_l, zsc_l = _proj_in(xl, mod_l, norm1_g[i], w_in_bf)
        zrw_c, q_c, k_c, v_c, zsc_c = _proj_in(xc, mod_c, norm1_g[i], w_in_bf)
        y2_c, y2_l = _rwkv_state(_rwkv_prep(zrw_c, *rw), _rwkv_prep(zrw_l, *rw))
        yrw_l = _rwkv_out(y2_l, zrw_l, rw_g_up[i], rw_r_k[i], rw_lnx_g[i])
        yna_l = _na_attention(q_l, k_l, v_l, k_c, v_c, _na_bias_table(na_rpb[i], rows, kr))
        ysc_l = _short_conv(zsc_l, sc_conv_w[i])
        xl = _proj_out(yrw_l, yna_l, ysc_l, w_out_bf, xl, mod_l)

        qw_hi, qw_lo = _split_bf16(peer_q_w[i])
        u_bf = peer_u[i].astype(BF16)
        vt_bf = peer_v[i].astype(BF16).T
        peer_args = (norm2_g[i], qw_hi, qw_lo, peer_sub_keys[i], u_bf, vt_bf, final_g)
        xl = _peer(xl, mod_l, *peer_args, final_norm=not need_ctx)
        if need_ctx:
            yrw_c = _rwkv_out(y2_c, zrw_c, rw_g_up[i], rw_r_k[i], rw_lnx_g[i])
            yna_c = _ctx_attention(q_c, k_c, v_c)
            ysc_c = _short_conv(zsc_c, sc_conv_w[i])
            xc = _proj_out(yrw_c, yna_c, ysc_c, w_out_bf, xc, mod_c)
            xc = _peer(xc, mod_c, *peer_args, final_norm=False)
    return xl
```

```python
import functools

import numpy as np
import jax
import jax.numpy as jnp
from jax import lax
from jax.experimental import pallas as pl
from jax.experimental.pallas import tpu as pltpu

F32 = jnp.float32
BF16 = jnp.bfloat16

HEAD_DIM = 64
GRID_W = 64
RW_H = 4
RW_W = RW_H * HEAD_DIM
RW_LW = 64
RW_LA = 64
RW_LG = 128
RW_COLS = 3 * RW_W + 2 * RW_LW + 2 * RW_LA + RW_LG
NA_H = 8
NA_W = NA_H * HEAD_DIM
NA_KR = 8
NA_KC = 16
SC_W = 256
PEER_H = 8
PEER_NKEYS = 128
PEER_TOPK = 16
NORM_EPS = 1e-6
GN_EPS = 64e-5
NEG_INF = -1e30
RW_CHUNK = 64

VMEM_LIMIT = 52 * 1024 * 1024

_CAND = [(a, b) for a in range(PEER_TOPK) for b in range(PEER_TOPK) if (a + 1) * (b + 1) <= PEER_TOPK]
_NCAND = len(_CAND)
_NCAND_PAD = -(-_NCAND // 8) * 8


def _cparams(*sem):
    return pltpu.CompilerParams(dimension_semantics=sem, vmem_limit_bytes=VMEM_LIMIT)


def _mm(a, b, passes=1, nt=False):
    dims = (((1,), (1,)), ((), ())) if nt else (((1,), (0,)), ((), ()))
    dg = functools.partial(lax.dot_general, dimension_numbers=dims, preferred_element_type=F32)
    if passes == 6:
        return dg(a.astype(F32), b.astype(F32), precision=lax.Precision.HIGHEST)
    a_hi = a.astype(BF16)
    b_hi = b.astype(BF16)
    out = dg(a_hi, b_hi)
    if passes == 3:
        a_lo = (a - a_hi.astype(F32)).astype(BF16)
        b_lo = (b - b_hi.astype(F32)).astype(BF16)
        out = out + dg(a_lo, b_hi) + dg(a_hi, b_lo)
    return out


def _sigmoid(x):
    return 1.0 / (1.0 + jnp.exp(-x))


def _norm_mod(x, g, scale, shift):
    ms = jnp.mean(x * x, axis=-1, keepdims=True)
    y = x * lax.rsqrt(ms + NORM_EPS) * g
    return y * (1.0 + scale) + shift


def _ada_kernel(c_ref, w_ref, b_ref, o_ref):
    cc = c_ref[...]
    o_ref[0] = _mm(cc * _sigmoid(cc), w_ref[0], 6) + b_ref[0]


def _ada(cc, ada_w, ada_b):
    L, D, D6 = ada_w.shape
    R = cc.shape[0]
    tn = 1536
    return pl.pallas_call(
        _ada_kernel,
        grid=(L, D6 // tn),
        in_specs=[
            pl.BlockSpec((R, D), lambda l, j: (0, 0)),
            pl.BlockSpec((1, D, tn), lambda l, j: (l, 0, j)),
            pl.BlockSpec((1, 1, tn), lambda l, j: (l, 0, j)),
        ],
        out_specs=pl.BlockSpec((1, R, tn), lambda l, j: (l, 0, j)),
        out_shape=jax.ShapeDtypeStruct((L, R, D6), F32),
        compiler_params=_cparams("parallel", "parallel"),
        name="ada",
    )(cc, ada_w, ada_b.reshape(L, 1, D6))


def _proj_in_kernel(x_ref, mod_ref, g_ref, w_ref, zrw_ref, q_ref, k_ref, v_ref, zsc_ref):
    h = _norm_mod(x_ref[0], g_ref[...], mod_ref[0, 1], mod_ref[0, 0])
    z = jnp.dot(h.astype(BF16), w_ref[...], preferred_element_type=F32)
    o = RW_COLS
    zrw_ref[0] = z[:, :o]
    q_ref[0] = (z[:, o:o + NA_W] * (HEAD_DIM ** -0.5)).astype(BF16)
    k_ref[0] = z[:, o + NA_W:o + 2 * NA_W].astype(BF16)
    v_ref[0] = z[:, o + 2 * NA_W:o + 3 * NA_W].astype(BF16)
    zsc_ref[0] = z[:, o + 3 * NA_W:]


def _proj_in(x, mod, g, w_bf):
    B, T, D = x.shape
    tm = 256
    n_out = w_bf.shape[1]
    row = lambda b, t: (b, t, 0)
    return pl.pallas_call(
        _proj_in_kernel,
        grid=(B, T // tm),
        in_specs=[
            pl.BlockSpec((1, tm, D), row),
            pl.BlockSpec((1, 6, 1, D), lambda b, t: (b, 0, 0, 0)),
            pl.BlockSpec((1, D), lambda b, t: (0, 0)),
            pl.BlockSpec((D, n_out), lambda b, t: (0, 0)),
        ],
        out_specs=[
            pl.BlockSpec((1, tm, RW_COLS), row),
            pl.BlockSpec((1, tm, NA_W), row),
            pl.BlockSpec((1, tm, NA_W), row),
            pl.BlockSpec((1, tm, NA_W), row),
            pl.BlockSpec((1, tm, 3 * SC_W), row),
        ],
        out_shape=[
            jax.ShapeDtypeStruct((B, T, RW_COLS), F32),
            jax.ShapeDtypeStruct((B, T, NA_W), BF16),
            jax.ShapeDtypeStruct((B, T, NA_W), BF16),
            jax.ShapeDtypeStruct((B, T, NA_W), BF16),
            jax.ShapeDtypeStruct((B, T, 3 * SC_W), F32),
        ],
        compiler_params=_cparams("parallel", "parallel"),
        name="proj_in",
    )(x, mod, g.reshape(1, D), w_bf)


RW_CHUNKS_PER_STEP = 2
RW_P_ALG = 1
RW_P_LORA = 3


def _rwkv_prep_kernel(z_ref, w0_ref, wup_ref, a0_ref, aup_ref, kk_ref, ka_ref,
                      phi_ref, zc_ref, wr_ref, yin_ref):
    C = RW_CHUNK
    W = RW_W
    assert C == HEAD_DIM
    N = RW_H * C
    row = lax.broadcasted_iota(jnp.int32, (C, C), 0)
    col = lax.broadcasted_iota(jnp.int32, (C, C), 1)
    brow = lax.broadcasted_iota(jnp.int32, (N, W), 0)
    bcol = lax.broadcasted_iota(jnp.int32, (N, W), 1)
    bd = (brow // C) == (bcol // HEAD_DIM)
    eye_j = lax.broadcasted_iota(jnp.int32, (HEAD_DIM, W), 0) == \
        lax.broadcasted_iota(jnp.int32, (HEAD_DIM, W), 1) % HEAD_DIM
    head_ones = bd.astype(F32)
    mm = functools.partial(_mm, passes=RW_P_ALG)
    mm_tn = lambda a, b: lax.dot_general(a.astype(BF16), b.astype(BF16), (((0,), (0,)), ((), ())),
                                         preferred_element_type=F32)
    collapse = lambda m: functools.reduce(lambda a, b: a + b, [m[h * C:(h + 1) * C] for h in range(RW_H)])

    def direction(ci, d):
        z = z_ref[0, ci * C:(ci + 1) * C, :]
        r = z[:, 0:W]
        k = z[:, W:2 * W]
        v = z[:, 2 * W:3 * W]
        zw = z[:, 3 * W:3 * W + 2 * RW_LW]
        za = z[:, 3 * W + 2 * RW_LW:3 * W + 2 * RW_LW + 2 * RW_LA]
        kk = k * kk_ref[...]
        kk = kk * lax.rsqrt(_mm(kk * kk, head_ones, 3) + 1e-12)
        lw = jnp.tanh(zw[:, d * RW_LW:(d + 1) * RW_LW])
        la = za[:, d * RW_LA:(d + 1) * RW_LA]
        w_raw = w0_ref[d:d + 1, :] + _mm(lw, wup_ref[d], RW_P_LORA)
        softplus = jnp.maximum(-w_raw, 0.0) + jnp.log(1.0 + jnp.exp(-jnp.abs(w_raw)))
        logw = -jnp.exp(-softplus - 0.5)
        a = _sigmoid(a0_ref[d:d + 1, :] + _mm(la, aup_ref[d], RW_P_LORA))
        k_d = k * (1.0 + (a - 1.0) * ka_ref[...])
        b_d = kk * a
        if d == 0:
            tri_incl, tri_strict = col <= row, col < row
        else:
            tri_incl, tri_strict = col >= row, col > row
        cum = _mm(tri_incl.astype(F32), logw, 6)
        tot = jnp.sum(logw, axis=0, keepdims=True)
        e_neg = jnp.exp(-cum)
        e_end = jnp.exp(tot - cum)
        a_t = -kk * jnp.exp(cum - logw)
        r_t = r * jnp.exp(cum)
        b_t = b_d * e_neg
        k_t = k_d * e_neg
        b_p = b_d * e_end
        k_p = k_d * e_end
        p_c = jnp.exp(tot)
        bdx = lambda x: jnp.where(bd, jnp.concatenate([x] * RW_H, axis=0), 0.0)
        a_bd, r_bd, v_bd = bdx(a_t), bdx(r_t), bdx(v)
        yield
        p4 = mm(jnp.concatenate([a_bd, r_bd], axis=0),
                jnp.concatenate([bdx(b_t), bdx(k_t)], axis=0), nt=True)
        yield
        tri_strict_bd = bd & ((bcol < brow) if d == 0 else (bcol > brow))
        tri_incl_bd = bd & ((bcol <= brow) if d == 0 else (bcol >= brow))
        l_ab = jnp.where(tri_strict_bd, p4[:N, :N], 0.0)
        l_ak = jnp.where(tri_strict_bd, p4[:N, N:], 0.0)
        m_rb = jnp.where(tri_incl_bd, p4[N:, :N], 0.0)
        m_rk = jnp.where(tri_incl_bd, p4[N:, N:], 0.0)
        t_inv = jnp.where(brow == bcol, 1.0, l_ab)
        pw = l_ab
        n = 2
        lakv = mm(l_ak, v_bd)
        mrkv = mm(m_rk, v_bd)
        kpv = mm_tn(bdx(k_p), v_bd)
        while n < C:
            pw = mm(pw, pw)
            yield
            t_inv = t_inv + mm(t_inv, pw)
            n *= 2
        yield
        x = mm(t_inv, jnp.concatenate([a_bd, lakv], axis=1))
        yield
        wy = mm(m_rb, x)
        pz = mm_tn(bdx(b_p), x)
        wr_ref[0, ci, d] = r_t + collapse(wy[:, :W])
        yin_ref[0, ci, d] = collapse(wy[:, W:] + mrkv)
        phi_ref[0, ci, d] = jnp.where(eye_j, p_c, 0.0) + collapse(pz[:, :W])
        zc_ref[0, ci, d] = collapse(pz[:, W:] + kpv)

    live = [direction(ci, d) for ci in range(z_ref.shape[1] // C) for d in range(2)]
    done = object()
    while live:
        live = [g for g in live if next(g, done) is not done]


def _rwkv_prep(zrw, w0, w_up, a0, a_up, k_k, k_a):
    B, T, _ = zrw.shape
    C = RW_CHUNK
    nch = T // C
    full = lambda a: pl.BlockSpec(a.shape, lambda b, c: (0,) * a.ndim)
    k_k = k_k.reshape(1, RW_W)
    k_a = k_a.reshape(1, RW_W)
    cps = RW_CHUNKS_PER_STEP
    ospec = pl.BlockSpec((1, cps, 2, HEAD_DIM, RW_W), lambda b, c: (b, c, 0, 0, 0))
    oshape = jax.ShapeDtypeStruct((B, nch, 2, HEAD_DIM, RW_W), F32)
    return pl.pallas_call(
        _rwkv_prep_kernel,
        grid=(B, nch // cps),
        in_specs=[pl.BlockSpec((1, cps * C, RW_COLS), lambda b, c: (b, c, 0)),
                  full(w0), full(w_up), full(a0), full(a_up), full(k_k), full(k_a)],
        out_specs=[ospec] * 4,
        out_shape=[oshape] * 4,
        compiler_params=_cparams("parallel", "parallel"),
        name="rwkv_prep",
    )(zrw, w0, w_up, a0, a_up, k_k, k_a)


RW_P_STATE = 3


def _rwkv_state_kernel(phic, zcc, wrc, yinc, phil, zcl, wrl, yinl, yc_ref, yl_ref):
    mm = functools.partial(_mm, passes=RW_P_STATE)

    N = RW_H * HEAD_DIM
    bd = (lax.broadcasted_iota(jnp.int32, (N, RW_W), 0) // HEAD_DIM) == \
        (lax.broadcasted_iota(jnp.int32, (N, RW_W), 1) // HEAD_DIM)

    def run(Hs, phi, zc, wr, yin, y_ref):
        nch = phi.shape[1]

        def body(s, Hs):
            new = []
            for d, H in enumerate(Hs):
                c = s if d == 0 else nch - 1 - s
                t0 = pl.multiple_of(c * RW_CHUNK, RW_CHUNK)
                h_bd = jnp.where(bd, jnp.concatenate([H] * RW_H, axis=0), 0.0)
                ops = jnp.concatenate([wr[0, c, d], phi[0, c, d]], axis=0)
                out = mm(ops, h_bd) + jnp.concatenate([yin[0, c, d], zc[0, c, d]], axis=0)
                y_ref[0, d, pl.ds(t0, RW_CHUNK), :] = out[:RW_CHUNK]
                new.append(out[RW_CHUNK:])
            return tuple(new)

        return lax.fori_loop(0, nch, body, Hs)

    Hs = (jnp.zeros((HEAD_DIM, RW_W), F32),) * 2
    Hs = run(Hs, phic, zcc, wrc, yinc, yc_ref)
    run(Hs, phil, zcl, wrl, yinl, yl_ref)


def _rwkv_state(ops_c, ops_l):
    B, nc = ops_c[0].shape[:2]
    nl = ops_l[0].shape[1]
    spec = lambda n: pl.BlockSpec((1, n, 2, HEAD_DIM, RW_W), lambda b: (b, 0, 0, 0, 0))
    yspec = lambda n: pl.BlockSpec((1, 2, n * RW_CHUNK, RW_W), lambda b: (b, 0, 0, 0))
    return pl.pallas_call(
        _rwkv_state_kernel,
        grid=(B,),
        in_specs=[spec(nc)] * 4 + [spec(nl)] * 4,
        out_specs=[yspec(nc), yspec(nl)],
        out_shape=[jax.ShapeDtypeStruct((B, 2, nc * RW_CHUNK, RW_W), F32),
                   jax.ShapeDtypeStruct((B, 2, nl * RW_CHUNK, RW_W), F32)],
        compiler_params=_cparams("parallel"),
        name="rwkv_state",
    )(*ops_c, *ops_l)


def _rwkv_out_kernel(yf_ref, yr_ref, z_ref, gup_ref, rk_ref, lnx_ref, o_ref):
    W = RW_W
    y = yf_ref[0, 0] + yr_ref[0, 0]
    z = z_ref[0]
    r, k, v = z[:, 0:W], z[:, W:2 * W], z[:, 2 * W:3 * W]
    zg = z[:, RW_COLS - RW_LG:RW_COLS]
    hrow = lax.broadcasted_iota(jnp.int32, (W, W), 0) // HEAD_DIM
    hcol = lax.broadcasted_iota(jnp.int32, (W, W), 1) // HEAD_DIM
    head_ones = (hrow == hcol).astype(F32)
    inv = 1.0 / HEAD_DIM
    mu = _mm(y, head_ones, 3) * inv
    yc = y - mu
    var = _mm(yc * yc, head_ones, 3) * inv
    yn = yc * lax.rsqrt(var + GN_EPS) * lnx_ref[...]
    bonus = _mm(r * k * rk_ref[...], head_ones, 3)
    yn = yn + bonus * v
    gate = _mm(_sigmoid(zg), gup_ref[...], 3)
    o_ref[0] = yn * gate


def _rwkv_out(y2, zrw, g_up, r_k, lnx_g):
    B, _, T, W = y2.shape
    tm = 256
    return pl.pallas_call(
        _rwkv_out_kernel,
        grid=(B, T // tm),
        in_specs=[
            pl.BlockSpec((1, 1, tm, W), lambda b, t: (b, 0, t, 0)),
            pl.BlockSpec((1, 1, tm, W), lambda b, t: (b, 1, t, 0)),
            pl.BlockSpec((1, tm, RW_COLS), lambda b, t: (b, t, 0)),
            pl.BlockSpec((RW_LG, W), lambda b, t: (0, 0)),
            pl.BlockSpec((1, W), lambda b, t: (0, 0)),
            pl.BlockSpec((1, W), lambda b, t: (0, 0)),
        ],
        out_specs=pl.BlockSpec((1, tm, W), lambda b, t: (b, t, 0)),
        out_shape=jax.ShapeDtypeStruct((B, T, W), F32),
        compiler_params=_cparams("parallel", "parallel"),
        name="rwkv_out",
    )(y2, y2, zrw, g_up, r_k.reshape(1, W), lnx_g.reshape(1, W))


def _softmax_av_steps(scores, values, store):
    m = functools.reduce(jnp.maximum, [jnp.max(s, axis=-1, keepdims=True) for s in scores])
    yield
    ps = [jnp.exp(s - m) for s in scores]
    l = functools.reduce(lambda a, b: a + b, [jnp.sum(p, axis=-1, keepdims=True) for p in ps])
    yield
    o = functools.reduce(lambda a, b: a + b,
                         [jnp.dot(p.astype(BF16), v, preferred_element_type=F32) for p, v in zip(ps, values)])
    yield
    store(o / l)


def _interleave(chains):
    live = list(chains)
    done = object()
    while live:
        live = [g for g in live if next(g, done) is not done]


NA_HEADS_INTERLEAVED = 8


def _na_kernel(q_ref, k_ref, v_ref, kc_ref, vc_ref, bias_ref, o_ref, *, kr):
    r = pl.program_id(1)
    rows = pl.num_programs(1)
    start = pl.multiple_of(jnp.clip(r - NA_KR // 2, 0, rows - kr) * GRID_W, GRID_W)
    nwin = kr * GRID_W

    def head(h):
        sl = slice(h * HEAD_DIM, (h + 1) * HEAD_DIM)
        q = q_ref[0, :, sl]
        kb = k_ref[0, pl.ds(start, nwin), sl]
        vb = v_ref[0, pl.ds(start, nwin), sl]
        s_win = _mm(q, kb, nt=True) + bias_ref[0, h]
        s_ctx = _mm(q, kc_ref[0, :, sl], nt=True)
        yield

        def store(o):
            o_ref[0, :, sl] = o

        yield from _softmax_av_steps([s_win, s_ctx], [vb, vc_ref[0, :, sl]], store)

    for h0 in range(0, NA_H, NA_HEADS_INTERLEAVED):
        _interleave(head(h) for h in range(h0, h0 + NA_HEADS_INTERLEAVED))


def _na_bias_table(rpb, rows, kr):
    ci = np.arange(GRID_W)
    c0 = np.clip(ci - NA_KC // 2, 0, GRID_W - NA_KC)
    col_ok = (ci[None, :] >= c0[:, None]) & (ci[None, :] < c0[:, None] + NA_KC)
    dc = np.clip(ci[None, :] - ci[:, None] + (NA_KC - 1), 0, 2 * NA_KC - 2)
    n_off = NA_KR
    dr = np.arange(kr)[None, :] - np.arange(n_off)[:, None] + (NA_KR - 1)
    dr_ok = (dr >= 0) & (dr < 2 * NA_KR - 1)
    sel_r = (dr[:, :, None] == np.arange(2 * NA_KR - 1)).astype(np.float32)
    sel_c = (dc[:, :, None] == np.arange(2 * NA_KC - 1)).astype(np.float32)
    bias = jnp.einsum('hrc,okr,qwc->ohqkw', rpb.astype(F32), sel_r, sel_c, precision=lax.Precision.HIGHEST)
    ok = dr_ok[:, None, None, :, None] & col_ok[None, None, :, None, :]
    bias = jnp.where(ok, bias, NEG_INF)
    return bias.reshape(n_off, NA_H, GRID_W, kr * GRID_W)


def _na_attention(q, k, v, kc, vc, bias_tab):
    B, S, W = q.shape
    Cn = kc.shape[1]
    rows = S // GRID_W
    kr = min(NA_KR, rows)
    nwin = kr * GRID_W

    def bias_idx(b, r):
        return (r - jnp.clip(r - NA_KR // 2, 0, rows - kr), 0, 0, 0)

    return pl.pallas_call(
        functools.partial(_na_kernel, kr=kr),
        grid=(B, rows),
        in_specs=[
            pl.BlockSpec((1, GRID_W, W), lambda b, r: (b, r, 0)),
            pl.BlockSpec((1, S, W), lambda b, r: (b, 0, 0)),
            pl.BlockSpec((1, S, W), lambda b, r: (b, 0, 0)),
            pl.BlockSpec((1, Cn, W), lambda b, r: (b, 0, 0)),
            pl.BlockSpec((1, Cn, W), lambda b, r: (b, 0, 0)),
            pl.BlockSpec((1, NA_H, GRID_W, nwin), bias_idx),
        ],
        out_specs=pl.BlockSpec((1, GRID_W, W), lambda b, r: (b, r, 0)),
        out_shape=jax.ShapeDtypeStruct((B, S, W), F32),
        compiler_params=_cparams("parallel", "arbitrary"),
        name="na_attn",
    )(q, k, v, kc, vc, bias_tab)


def _ctx_attn_kernel(q_ref, k_ref, v_ref, o_ref):
    def head(h):
        sl = slice(h * HEAD_DIM, (h + 1) * HEAD_DIM)
        s = _mm(q_ref[0, :, sl], k_ref[0, :, sl], nt=True)
        yield

        def store(o):
            o_ref[0, :, sl] = o

        yield from _softmax_av_steps([s], [v_ref[0, :, sl]], store)

    for h0 in range(0, NA_H, NA_HEADS_INTERLEAVED):
        _interleave(head(h) for h in range(h0, h0 + NA_HEADS_INTERLEAVED))


def _ctx_attention(q, k, v):
    B, Cn, W = q.shape
    spec = pl.BlockSpec((1, Cn, W), lambda b: (b, 0, 0))
    return pl.pallas_call(
        _ctx_attn_kernel,
        grid=(B,),
        in_specs=[spec] * 3,
        out_specs=spec,
        out_shape=jax.ShapeDtypeStruct((B, Cn, W), F32),
        compiler_params=_cparams("parallel"),
        name="ctx_attn",
    )(q, k, v)


def _conv_kernel(z_ref, w_ref, o_ref):
    T = z_ref.shape[1]
    z = z_ref[0]
    zb, zc, zx = z[:, :SC_W], z[:, SC_W:2 * SC_W], z[:, 2 * SC_W:]
    u = zc * zx
    t = lax.broadcasted_iota(jnp.int32, u.shape, 0)
    prev = jnp.where(t == 0, 0.0, pltpu.roll(u, 1, 0))
    nxt = jnp.where(t == T - 1, 0.0, pltpu.roll(u, T - 1, 0))
    o_ref[0] = zb * (prev * w_ref[0:1, :] + u * w_ref[1:2, :] + nxt * w_ref[2:3, :])


def _short_conv(zsc, w):
    B, T, _ = zsc.shape
    return pl.pallas_call(
        _conv_kernel,
        grid=(B,),
        in_specs=[pl.BlockSpec((1, T, 3 * SC_W), lambda b: (b, 0, 0)),
                  pl.BlockSpec((3, SC_W), lambda b: (0, 0))],
        out_specs=pl.BlockSpec((1, T, SC_W), lambda b: (b, 0, 0)),
        out_shape=jax.ShapeDtypeStruct((B, T, SC_W), F32),
        compiler_params=_cparams("parallel"),
        name="short_conv",
    )(zsc, w)


def _proj_out_kernel(yrw_ref, yna_ref, ysc_ref, w_ref, x_ref, mod_ref, o_ref):
    a, b = RW_W, RW_W + NA_W
    y = (jnp.dot(yrw_ref[0].astype(BF16), w_ref[:a, :], preferred_element_type=F32)
         + jnp.dot(yna_ref[0].astype(BF16), w_ref[a:b, :], preferred_element_type=F32)
         + jnp.dot(ysc_ref[0].astype(BF16), w_ref[b:, :], preferred_element_type=F32))
    o_ref[0] = x_ref[0] + mod_ref[0, 2] * y


def _proj_out(yrw, yna, ysc, w_bf, x, mod):
    B, T, D = x.shape
    tm = 256
    row = lambda b, t: (b, t, 0)
    return pl.pallas_call(
        _proj_out_kernel,
        grid=(B, T // tm),
        in_specs=[
            pl.BlockSpec((1, tm, RW_W), row),
            pl.BlockSpec((1, tm, NA_W), row),
            pl.BlockSpec((1, tm, SC_W), row),
            pl.BlockSpec(w_bf.shape, lambda b, t: (0, 0)),
            pl.BlockSpec((1, tm, D), row),
            pl.BlockSpec((1, 6, 1, D), lambda b, t: (b, 0, 0, 0)),
        ],
        out_specs=pl.BlockSpec((1, tm, D), row),
        out_shape=jax.ShapeDtypeStruct((B, T, D), F32),
        compiler_params=_cparams("parallel", "parallel"),
        name="proj_out",
    )(yrw, yna, ysc, w_bf, x, mod)


PEER_P_ROUTE = 3


def _peer_query_kernel(x_ref, mod_ref, g_ref, wh_ref, wl_ref, sk_ref, ht_ref, s_ref):
    h = _norm_mod(x_ref[0], g_ref[...], mod_ref[0, 4], mod_ref[0, 3])
    h_hi = h.astype(BF16)
    ht_ref[0] = h.T.astype(BF16)
    q = jnp.dot(h_hi, wh_ref[...], preferred_element_type=F32)
    if PEER_P_ROUTE == 3:
        h_lo = (h - h_hi.astype(F32)).astype(BF16)
        q = q + jnp.dot(h_lo, wh_ref[...], preferred_element_type=F32) \
            + jnp.dot(h_hi, wl_ref[...], preferred_element_type=F32)
    dq = sk_ref.shape[2]
    for hp in range(sk_ref.shape[0]):
        s_ref[0, hp] = _mm(sk_ref[hp], q[:, hp * dq:(hp + 1) * dq], PEER_P_ROUTE, nt=True)


def _peer_query(x, mod, g, qw_hi, qw_lo, sub_keys):
    B, T, D = x.shape
    tm = 256
    nq = qw_hi.shape[1]
    sk = sub_keys.reshape(2 * PEER_H, PEER_NKEYS, -1)
    return pl.pallas_call(
        _peer_query_kernel,
        grid=(B, T // tm),
        in_specs=[
            pl.BlockSpec((1, tm, D), lambda b, t: (b, t, 0)),
            pl.BlockSpec((1, 6, 1, D), lambda b, t: (b, 0, 0, 0)),
            pl.BlockSpec((1, D), lambda b, t: (0, 0)),
            pl.BlockSpec((D, nq), lambda b, t: (0, 0)),
            pl.BlockSpec((D, nq), lambda b, t: (0, 0)),
            pl.BlockSpec(sk.shape, lambda b, t: (0, 0, 0)),
        ],
        out_specs=[pl.BlockSpec((1, D, tm), lambda b, t: (b, 0, t)),
                   pl.BlockSpec((1, 2 * PEER_H, PEER_NKEYS, tm), lambda b, t: (b, 0, 0, t))],
        out_shape=[jax.ShapeDtypeStruct((B, D, T), BF16),
                   jax.ShapeDtypeStruct((B, 2 * PEER_H, PEER_NKEYS, T), F32)],
        compiler_params=_cparams("parallel", "parallel"),
        name="peer_query",
    )(x, mod, g.reshape(1, D), qw_hi, qw_lo, sk)


def _extract_top(s, n, on_pick):
    rows = lax.broadcasted_iota(jnp.int32, s.shape, 0).astype(F32)
    big = float(s.shape[0])
    for it in range(n):
        m = jnp.max(s, axis=0, keepdims=True)
        idx = jnp.min(jnp.where(s == m, rows, big), axis=0, keepdims=True)
        sel = rows == idx
        on_pick(it, m, sel)
        s = jnp.where(sel, -jnp.inf, s)


def _dup_bf16(x):
    bits = lax.bitcast_convert_type(x.astype(BF16).astype(F32), jnp.uint32)
    return bits | (bits >> 16)


def _peer_route_kernel(s_ref, rank2_ref, cnt_ref, e1_ref, e2_ref, v1_scr, v2_scr, cand_scr):
    tn = s_ref.shape[3]
    for h in range(PEER_H):
        s1 = s_ref[0, 2 * h]
        s2 = s_ref[0, 2 * h + 1]
        ranks = []
        for s, scr in ((s1, v1_scr), (s2, v2_scr)):
            rank = [jnp.full(s.shape, float(PEER_NKEYS), F32)]

            def pick(it, m, sel, scr=scr, rank=rank):
                scr[it:it + 1, :] = m
                rank[0] = jnp.where(sel, float(it), rank[0])

            _extract_top(s, PEER_TOPK, pick)
            ranks.append(rank[0])
        rank1, rank2 = ranks
        cand_scr[...] = jnp.full(cand_scr.shape, -jnp.inf, F32)
        for ci, (a, b) in enumerate(_CAND):
            cand_scr[ci:ci + 1, :] = v1_scr[a:a + 1, :] + v2_scr[b:b + 1, :]
        picked = [jnp.zeros(cand_scr.shape, F32), jnp.zeros((1, tn), F32), None]

        def pick_c(it, m, sel, picked=picked):
            if it == 0:
                picked[2] = m
            picked[0] = jnp.where(sel, 1.0, picked[0])
            picked[1] = picked[1] + jnp.exp(m - picked[2])

        _extract_top(cand_scr[...], PEER_TOPK, pick_c)
        selmask, z = picked[0], picked[1]
        ca = lax.broadcasted_iota(jnp.int32, (PEER_TOPK, _NCAND_PAD), 0)
        cj = lax.broadcasted_iota(jnp.int32, (PEER_TOPK, _NCAND_PAD), 1)
        row_of = functools.reduce(jnp.logical_or, [(ca == a) & (cj == ci) for ci, (a, _) in enumerate(_CAND)])
        cnt_a = jnp.dot(row_of.astype(BF16), selmask.astype(BF16), preferred_element_type=F32)
        cnt = jnp.zeros(s1.shape, F32)
        for a in range(PEER_TOPK):
            cnt = jnp.where(rank1 == float(a), cnt_a[a:a + 1, :], cnt)
        rank2_ref[0, h] = rank2.astype(BF16)
        cnt_ref[0, h] = _dup_bf16(cnt)
        e1_ref[0, h] = _dup_bf16(jnp.exp(s1 - v1_scr[0:1, :]) / z)
        e2_ref[0, h] = jnp.exp(s2 - v2_scr[0:1, :]).astype(BF16)


def _peer_route(sT):
    B, _, NK, T = sT.shape
    tn = 256
    ospec = pl.BlockSpec((1, PEER_H, NK, tn), lambda b, t: (b, 0, 0, t))
    oshape = lambda dt: jax.ShapeDtypeStruct((B, PEER_H, NK, T), dt)
    return pl.pallas_call(
        _peer_route_kernel,
        grid=(B, T // tn),
        in_specs=[pl.BlockSpec((1, 2 * PEER_H, NK, tn), lambda b, t: (b, 0, 0, t))],
        out_specs=[ospec] * 4,
        out_shape=[oshape(BF16), oshape(jnp.uint32), oshape(jnp.uint32), oshape(BF16)],
        scratch_shapes=[pltpu.VMEM((PEER_TOPK, tn), F32), pltpu.VMEM((PEER_TOPK, tn), F32),
                        pltpu.VMEM((_NCAND_PAD, tn), F32)],
        compiler_params=_cparams("parallel", "parallel"),
        name="peer_route",
    )(sT)


_GELU_C = float(np.sqrt(2.0 / np.pi))
PEER_SUB = 256


def _gelu_tanh(x):
    hx = 0.5 * x
    return hx + hx * jnp.tanh(x * (_GELU_C + (_GELU_C * 0.044715) * (x * x)))


def _row_tile(words):
    return pltpu.bitcast(jnp.broadcast_to(words, (PEER_NKEYS // 2, words.shape[1])), BF16)


def _peer_expert_kernel(ht_ref, rank2_ref, cnt_ref, e1_ref, e2_ref, u_ref, vt_ref, x_ref, mod_ref,
                        fg_ref, o_ref, acc_ref, g_scr, *, te, final_norm):
    e = pl.program_id(2)

    @pl.when(e == 0)
    def _():
        acc_ref[...] = jnp.zeros(acc_ref.shape, F32)

    ht = ht_ref[0]
    rows = te // PEER_NKEYS
    for ii in range(rows):
        i = e * rows + ii
        g = None
        for hd in range(PEER_H):
            cnt_row = _row_tile(cnt_ref[0, hd, pl.ds(i, 1), :])
            e1_row = _row_tile(e1_ref[0, hd, pl.ds(i, 1), :])
            gh = jnp.where(rank2_ref[0, hd] < cnt_row, e2_ref[0, hd] * e1_row, jnp.zeros((), BF16))
            g = gh if g is None else g + gh
        g_scr[ii * PEER_NKEYS:(ii + 1) * PEER_NKEYS, :] = g
    for sb in range(te // PEER_SUB):
        es = slice(sb * PEER_SUB, (sb + 1) * PEER_SUB)
        a = jnp.dot(u_ref[es, :], ht, preferred_element_type=F32)
        g_scr[es, :] = _gelu_tanh(a.astype(BF16)) * g_scr[es, :]
    acc_ref[...] += jnp.dot(vt_ref[...], g_scr[...], preferred_element_type=F32)

    @pl.when(e == pl.num_programs(2) - 1)
    def _():
        y = x_ref[0] + mod_ref[0, 5] * acc_ref[...].T
        if final_norm:
            y = y * lax.rsqrt(jnp.mean(y * y, axis=-1, keepdims=True) + NORM_EPS) * fg_ref[...]
        o_ref[0] = y


def _peer_expert(h2, route, u_bf, vt_bf, x, mod, final_g, final_norm):
    B, T, D = x.shape
    tn = min(512, T)
    te = 2048
    ne = u_bf.shape[0] // te
    rspec = pl.BlockSpec((1, PEER_H, PEER_NKEYS, tn), lambda b, t, e: (b, 0, 0, t))
    return pl.pallas_call(
        functools.partial(_peer_expert_kernel, te=te, final_norm=final_norm),
        grid=(B, T // tn, ne),
        in_specs=[
            pl.BlockSpec((1, D, tn), lambda b, t, e: (b, 0, t)),
            rspec, rspec, rspec, rspec,
            pl.BlockSpec((te, D), lambda b, t, e: (e, 0)),
            pl.BlockSpec((D, te), lambda b, t, e: (0, e)),
            pl.BlockSpec((1, tn, D), lambda b, t, e: (b, t, 0)),
            pl.BlockSpec((1, 6, 1, D), lambda b, t, e: (b, 0, 0, 0)),
            pl.BlockSpec((1, D), lambda b, t, e: (0, 0)),
        ],
        out_specs=pl.BlockSpec((1, tn, D), lambda b, t, e: (b, t, 0)),
        out_shape=jax.ShapeDtypeStruct((B, T, D), F32),
        scratch_shapes=[pltpu.VMEM((D, tn), F32), pltpu.VMEM((te, tn), BF16)],
        compiler_params=_cparams("parallel", "parallel", "arbitrary"),
        name="peer_expert",
    )(h2, *route, u_bf, vt_bf, x, mod, final_g.reshape(1, D))


def _peer(x, mod, g, qw_hi, qw_lo, sub_keys, u_bf, vt_bf, final_g, final_norm):
    h2, sT = _peer_query(x, mod, g, qw_hi, qw_lo, sub_keys)
    route = _peer_route(sT)
    return _peer_expert(h2, route, u_bf, vt_bf, x, mod, final_g, final_norm)


def _split_bf16(w):
    hi = w.astype(BF16)
    return hi, (w - hi.astype(F32)).astype(BF16)


def kernel(x, c, ctx, c_ctx, ada_w, ada_b, norm1_g, norm2_g, w_in, rw_w0, rw_w_up, rw_a0, rw_a_up, rw_g_up,
           rw_k_k, rw_k_a, rw_r_k, rw_lnx_g, na_rpb, sc_conv_w, w_out, peer_q_w, peer_sub_keys, peer_u,
           peer_v, final_g):
    B, S, D = x.shape
    depth = ada_w.shape[0]
    rows = S // GRID_W
    kr = min(NA_KR, rows)
    n_mod = -(-(B + 1) // 8) * 8
    cc = jnp.concatenate([c, c_ctx[None, :], jnp.zeros((n_mod - B - 1, D), F32)], axis=0)
    mod = _ada(cc, ada_w, ada_b)
    xl, xc = x, ctx
    for i in range(depth):
        need_ctx = i < depth - 1
        mod_l = mod[i, :B].reshape(B, 6, 1, D)
        mod_c = jnp.broadcast_to(mod[i, B].reshape(1, 6, 1, D), (B, 6, 1, D))
        w_in_bf = w_in[i].astype(BF16)
        w_out_bf = w_out[i].astype(BF16)
        rw = (rw_w0[i], rw_w_up[i], rw_a0[i], rw_a_up[i], rw_k_k[i], rw_k_a[i])

        zrw_l, q_l, k_l, v_l, zsc_l = _proj_in(xl, mod_l, norm1_g[i], w_in_bf)
        zrw_c, q_c, k_c, v_c, zsc_c = _proj_in(xc, mod_c, norm1_g[i], w_in_bf)
        y2_c, y2_l = _rwkv_state(_rwkv_prep(zrw_c, *rw), _rwkv_prep(zrw_l, *rw))
        yrw_l = _rwkv_out(y2_l, zrw_l, rw_g_up[i], rw_r_k[i], rw_lnx_g[i])
        yna_l = _na_attention(q_l, k_l, v_l, k_c, v_c, _na_bias_table(na_rpb[i], rows, kr))
        ysc_l = _short_conv(zsc_l, sc_conv_w[i])
        xl = _proj_out(yrw_l, yna_l, ysc_l, w_out_bf, xl, mod_l)

        qw_hi, qw_lo = _split_bf16(peer_q_w[i])
        u_bf = peer_u[i].astype(BF16)
        vt_bf = peer_v[i].astype(BF16).T
        peer_args = (norm2_g[i], qw_hi, qw_lo, peer_sub_keys[i], u_bf, vt_bf, final_g)
        xl = _peer(xl, mod_l, *peer_args, final_norm=not need_ctx)
        if need_ctx:
            yrw_c = _rwkv_out(y2_c, zrw_c, rw_g_up[i], rw_r_k[i], rw_lnx_g[i])
            yna_c = _ctx_attention(q_c, k_c, v_c)
            ysc_c = _short_conv(zsc_c, sc_conv_w[i])
            xc = _proj_out(yrw_c, yna_c, ysc_c, w_out_bf, xc, mod_c)
            xc = _peer(xc, mod_c, *peer_args, final_norm=False)
    return xl
```

```python
import functools

import numpy as np
import jax
import jax.numpy as jnp
from jax import lax
from jax.experimental import pallas as pl
from jax.experimental.pallas import tpu as pltpu

F32 = jnp.float32
BF16 = jnp.bfloat16

HEAD_DIM = 64
GRID_W = 64
RW_H = 4
RW_W = RW_H * HEAD_DIM
RW_LW = 64
RW_LA = 64
RW_LG = 128
RW_COLS = 3 * RW_W + 2 * RW_LW + 2 * RW_LA + RW_LG
NA_H = 8
NA_W = NA_H * HEAD_DIM
NA_KR = 8
NA_KC = 16
SC_W = 256
PEER_H = 8
PEER_NKEYS = 128
PEER_TOPK = 16
NORM_EPS = 1e-6
GN_EPS = 64e-5
NEG_INF = -1e30
RW_CHUNK = 64

VMEM_LIMIT = 52 * 1024 * 1024

_CAND = [(a, b) for a in range(PEER_TOPK) for b in range(PEER_TOPK) if (a + 1) * (b + 1) <= PEER_TOPK]
_NCAND = len(_CAND)
_NCAND_PAD = -(-_NCAND // 8) * 8


def _cparams(*sem):
    return pltpu.CompilerParams(dimension_semantics=sem, vmem_limit_bytes=VMEM_LIMIT)


def _mm(a, b, passes=1, nt=False):
    dims = (((1,), (1,)), ((), ())) if nt else (((1,), (0,)), ((), ()))
    dg = functools.partial(lax.dot_general, dimension_numbers=dims, preferred_element_type=F32)
    if passes == 6:
        return dg(a.astype(F32), b.astype(F32), precision=lax.Precision.HIGHEST)
    a_hi = a.astype(BF16)
    b_hi = b.astype(BF16)
    out = dg(a_hi, b_hi)
    if passes == 3:
        a_lo = (a - a_hi.astype(F32)).astype(BF16)
        b_lo = (b - b_hi.astype(F32)).astype(BF16)
        out = out + dg(a_lo, b_hi) + dg(a_hi, b_lo)
    return out


def _sigmoid(x):
    return 1.0 / (1.0 + jnp.exp(-x))


def _norm_mod(x, g, scale, shift):
    ms = jnp.mean(x * x, axis=-1, keepdims=True)
    y = x * lax.rsqrt(ms + NORM_EPS) * g
    return y * (1.0 + scale) + shift


def _ada_kernel(c_ref, w_ref, b_ref, o_ref):
    cc = c_ref[...]
    o_ref[0] = _mm(cc * _sigmoid(cc), w_ref[0], 6) + b_ref[0]


def _ada(cc, ada_w, ada_b):
    L, D, D6 = ada_w.shape
    R = cc.shape[0]
    tn = 1536
    return pl.pallas_call(
        _ada_kernel,
        grid=(L, D6 // tn),
        in_specs=[
            pl.BlockSpec((R, D), lambda l, j: (0, 0)),
            pl.BlockSpec((1, D, tn), lambda l, j: (l, 0, j)),
            pl.BlockSpec((1, 1, tn), lambda l, j: (l, 0, j)),
        ],
        out_specs=pl.BlockSpec((1, R, tn), lambda l, j: (l, 0, j)),
        out_shape=jax.ShapeDtypeStruct((L, R, D6), F32),
        compiler_params=_cparams("parallel", "parallel"),
        name="ada",
    )(cc, ada_w, ada_b.reshape(L, 1, D6))


def _proj_in_kernel(x_ref, mod_ref, g_ref, w_ref, zrw_ref, q_ref, k_ref, v_ref, zsc_ref):
    h = _norm_mod(x_ref[0], g_ref[...], mod_ref[0, 1], mod_ref[0, 0])
    z = jnp.dot(h.astype(BF16), w_ref[...], preferred_element_type=F32)
    o = RW_COLS
    zrw_ref[0] = z[:, :o]
    q_ref[0] = (z[:, o:o + NA_W] * (HEAD_DIM ** -0.5)).astype(BF16)
    k_ref[0] = z[:, o + NA_W:o + 2 * NA_W].astype(BF16)
    v_ref[0] = z[:, o + 2 * NA_W:o + 3 * NA_W].astype(BF16)
    zsc_ref[0] = z[:, o + 3 * NA_W:]


def _proj_in(x, mod, g, w_bf):
    B, T, D = x.shape
    tm = 256
    n_out = w_bf.shape[1]
    row = lambda b, t: (b, t, 0)
    return pl.pallas_call(
        _proj_in_kernel,
        grid=(B, T // tm),
        in_specs=[
            pl.BlockSpec((1, tm, D), row),
            pl.BlockSpec((1, 6, 1, D), lambda b, t: (b, 0, 0, 0)),
            pl.BlockSpec((1, D), lambda b, t: (0, 0)),
            pl.BlockSpec((D, n_out), lambda b, t: (0, 0)),
        ],
        out_specs=[
            pl.BlockSpec((1, tm, RW_COLS), row),
            pl.BlockSpec((1, tm, NA_W), row),
            pl.BlockSpec((1, tm, NA_W), row),
            pl.BlockSpec((1, tm, NA_W), row),
            pl.BlockSpec((1, tm, 3 * SC_W), row),
        ],
        out_shape=[
            jax.ShapeDtypeStruct((B, T, RW_COLS), F32),
            jax.ShapeDtypeStruct((B, T, NA_W), BF16),
            jax.ShapeDtypeStruct((B, T, NA_W), BF16),
            jax.ShapeDtypeStruct((B, T, NA_W), BF16),
            jax.ShapeDtypeStruct((B, T, 3 * SC_W), F32),
        ],
        compiler_params=_cparams("parallel", "parallel"),
        name="proj_in",
    )(x, mod, g.reshape(1, D), w_bf)


RW_CHUNKS_PER_STEP = 2
RW_P_ALG = 1
RW_P_LORA = 3


def _rwkv_prep_kernel(z_ref, w0_ref, wup_ref, a0_ref, aup_ref, kk_ref, ka_ref,
                      phi_ref, zc_ref, wr_ref, yin_ref):
    C = RW_CHUNK
    W = RW_W
    assert C == HEAD_DIM
    N = RW_H * C
    row = lax.broadcasted_iota(jnp.int32, (C, C), 0)
    col = lax.broadcasted_iota(jnp.int32, (C, C), 1)
    brow = lax.broadcasted_iota(jnp.int32, (N, W), 0)
    bcol = lax.broadcasted_iota(jnp.int32, (N, W), 1)
    bd = (brow // C) == (bcol // HEAD_DIM)
    eye_j = lax.broadcasted_iota(jnp.int32, (HEAD_DIM, W), 0) == \
        lax.broadcasted_iota(jnp.int32, (HEAD_DIM, W), 1) % HEAD_DIM
    head_ones = bd.astype(F32)
    mm = functools.partial(_mm, passes=RW_P_ALG)
    mm_tn = lambda a, b: lax.dot_general(a.astype(BF16), b.astype(BF16), (((0,), (0,)), ((), ())),
                                         preferred_element_type=F32)
    collapse = lambda m: functools.reduce(lambda a, b: a + b, [m[h * C:(h + 1) * C] for h in range(RW_H)])

    def direction(ci, d):
        z = z_ref[0, ci * C:(ci + 1) * C, :]
        r = z[:, 0:W]
        k = z[:, W:2 * W]
        v = z[:, 2 * W:3 * W]
        zw = z[:, 3 * W:3 * W + 2 * RW_LW]
        za = z[:, 3 * W + 2 * RW_LW:3 * W + 2 * RW_LW + 2 * RW_LA]
        kk = k * kk_ref[...]
        kk = kk * lax.rsqrt(_mm(kk * kk, head_ones, 3) + 1e-12)
        lw = jnp.tanh(zw[:, d * RW_LW:(d + 1) * RW_LW])
        la = za[:, d * RW_LA:(d + 1) * RW_LA]
        w_raw = w0_ref[d:d + 1, :] + _mm(lw, wup_ref[d], RW_P_LORA)
        softplus = jnp.maximum(-w_raw, 0.0) + jnp.log(1.0 + jnp.exp(-jnp.abs(w_raw)))
        logw = -jnp.exp(-softplus - 0.5)
        a = _sigmoid(a0_ref[d:d + 1, :] + _mm(la, aup_ref[d], RW_P_LORA))
        k_d = k * (1.0 + (a - 1.0) * ka_ref[...])
        b_d = kk * a
        if d == 0:
            tri_incl, tri_strict = col <= row, col < row
        else:
            tri_incl, tri_strict = col >= row, col > row
        cum = _mm(tri_incl.astype(F32), logw, 6)
        tot = jnp.sum(logw, axis=0, keepdims=True)
        e_neg = jnp.exp(-cum)
        e_end = jnp.exp(tot - cum)
        a_t = -kk * jnp.exp(cum - logw)
        r_t = r * jnp.exp(cum)
        b_t = b_d * e_neg
        k_t = k_d * e_neg
        b_p = b_d * e_end
        k_p = k_d * e_end
        p_c = jnp.exp(tot)
        bdx = lambda x: jnp.where(bd, jnp.concatenate([x] * RW_H, axis=0), 0.0)
        a_bd, r_bd, v_bd = bdx(a_t), bdx(r_t), bdx(v)
        yield
        p4 = mm(jnp.concatenate([a_bd, r_bd], axis=0),
                jnp.concatenate([bdx(b_t), bdx(k_t)], axis=0), nt=True)
        yield
        tri_strict_bd = bd & ((bcol < brow) if d == 0 else (bcol > brow))
        tri_incl_bd = bd & ((bcol <= brow) if d == 0 else (bcol >= brow))
        l_ab = jnp.where(tri_strict_bd, p4[:N, :N], 0.0)
        l_ak = jnp.where(tri_strict_bd, p4[:N, N:], 0.0)
        m_rb = jnp.where(tri_incl_bd, p4[N:, :N], 0.0)
        m_rk = jnp.where(tri_incl_bd, p4[N:, N:], 0.0)
        t_inv = jnp.where(brow == bcol, 1.0, l_ab)
        pw = l_ab
        n = 2
        lakv = mm(l_ak, v_bd)
        mrkv = mm(m_rk, v_bd)
        kpv = mm_tn(bdx(k_p), v_bd)
        while n < C:
            pw = mm(pw, pw)
            yield
            t_inv = t_inv + mm(t_inv, pw)
            n *= 2
        yield
        x = mm(t_inv, jnp.concatenate([a_bd, lakv], axis=1))
        yield
        wy = mm(m_rb, x)
        pz = mm_tn(bdx(b_p), x)
        wr_ref[0, ci, d] = r_t + collapse(wy[:, :W])
        yin_ref[0, ci, d] = collapse(wy[:, W:] + mrkv)
        phi_ref[0, ci, d] = jnp.where(eye_j, p_c, 0.0) + collapse(pz[:, :W])
        zc_ref[0, ci, d] = collapse(pz[:, W:] + kpv)

    live = [direction(ci, d) for ci in range(z_ref.shape[1] // C) for d in range(2)]
    done = object()
    while live:
        live = [g for g in live if next(g, done) is not done]


def _rwkv_prep(zrw, w0, w_up, a0, a_up, k_k, k_a):
    B, T, _ = zrw.shape
    C = RW_CHUNK
    nch = T // C
    full = lambda a: pl.BlockSpec(a.shape, lambda b, c: (0,) * a.ndim)
    k_k = k_k.reshape(1, RW_W)
    k_a = k_a.reshape(1, RW_W)
    cps = RW_CHUNKS_PER_STEP
    ospec = pl.BlockSpec((1, cps, 2, HEAD_DIM, RW_W), lambda b, c: (b, c, 0, 0, 0))
    oshape = jax.ShapeDtypeStruct((B, nch, 2, HEAD_DIM, RW_W), F32)
    return pl.pallas_call(
        _rwkv_prep_kernel,
        grid=(B, nch // cps),
        in_specs=[pl.BlockSpec((1, cps * C, RW_COLS), lambda b, c: (b, c, 0)),
                  full(w0), full(w_up), full(a0), full(a_up), full(k_k), full(k_a)],
        out_specs=[ospec] * 4,
        out_shape=[oshape] * 4,
        compiler_params=_cparams("parallel", "parallel"),
        name="rwkv_prep",
    )(zrw, w0, w_up, a0, a_up, k_k, k_a)


RW_P_STATE = 3


def _rwkv_state_kernel(phic, zcc, wrc, yinc, phil, zcl, wrl, yinl, yc_ref, yl_ref):
    mm = functools.partial(_mm, passes=RW_P_STATE)

    N = RW_H * HEAD_DIM
    bd = (lax.broadcasted_iota(jnp.int32, (N, RW_W), 0) // HEAD_DIM) == \
        (lax.broadcasted_iota(jnp.int32, (N, RW_W), 1) // HEAD_DIM)

    def run(Hs, phi, zc, wr, yin, y_ref):
        nch = phi.shape[1]

        def body(s, Hs):
            new = []
            for d, H in enumerate(Hs):
                c = s if d == 0 else nch - 1 - s
                t0 = pl.multiple_of(c * RW_CHUNK, RW_CHUNK)
                h_bd = jnp.where(bd, jnp.concatenate([H] * RW_H, axis=0), 0.0)
                ops = jnp.concatenate([wr[0, c, d], phi[0, c, d]], axis=0)
                out = mm(ops, h_bd) + jnp.concatenate([yin[0, c, d], zc[0, c, d]], axis=0)
                y_ref[0, d, pl.ds(t0, RW_CHUNK), :] = out[:RW_CHUNK]
                new.append(out[RW_CHUNK:])
            return tuple(new)

        return lax.fori_loop(0, nch, body, Hs)

    Hs = (jnp.zeros((HEAD_DIM, RW_W), F32),) * 2
    Hs = run(Hs, phic, zcc, wrc, yinc, yc_ref)
    run(Hs, phil, zcl, wrl, yinl, yl_ref)


def _rwkv_state(ops_c, ops_l):
    B, nc = ops_c[0].shape[:2]
    nl = ops_l[0].shape[1]
    spec = lambda n: pl.BlockSpec((1, n, 2, HEAD_DIM, RW_W), lambda b: (b, 0, 0, 0, 0))
    yspec = lambda n: pl.BlockSpec((1, 2, n * RW_CHUNK, RW_W), lambda b: (b, 0, 0, 0))
    return pl.pallas_call(
        _rwkv_state_kernel,
        grid=(B,),
        in_specs=[spec(nc)] * 4 + [spec(nl)] * 4,
        out_specs=[yspec(nc), yspec(nl)],
        out_shape=[jax.ShapeDtypeStruct((B, 2, nc * RW_CHUNK, RW_W), F32),
                   jax.ShapeDtypeStruct((B, 2, nl * RW_CHUNK, RW_W), F32)],
        compiler_params=_cparams("parallel"),
        name="rwkv_state",
    )(*ops_c, *ops_l)


def _rwkv_out_kernel(yf_ref, yr_ref, z_ref, gup_ref, rk_ref, lnx_ref, o_ref):
    W = RW_W
    y = yf_ref[0, 0] + yr_ref[0, 0]
    z = z_ref[0]
    r, k, v = z[:, 0:W], z[:, W:2 * W], z[:, 2 * W:3 * W]
    zg = z[:, RW_COLS - RW_LG:RW_COLS]
    hrow = lax.broadcasted_iota(jnp.int32, (W, W), 0) // HEAD_DIM
    hcol = lax.broadcasted_iota(jnp.int32, (W, W), 1) // HEAD_DIM
    head_ones = (hrow == hcol).astype(F32)
    inv = 1.0 / HEAD_DIM
    mu = _mm(y, head_ones, 3) * inv
    yc = y - mu
    var = _mm(yc * yc, head_ones, 3) * inv
    yn = yc * lax.rsqrt(var + GN_EPS) * lnx_ref[...]
    bonus = _mm(r * k * rk_ref[...], head_ones, 3)
    yn = yn + bonus * v
    gate = _mm(_sigmoid(zg), gup_ref[...], 3)
    o_ref[0] = yn * gate


def _rwkv_out(y2, zrw, g_up, r_k, lnx_g):
    B, _, T, W = y2.shape
    tm = 256
    return pl.pallas_call(
        _rwkv_out_kernel,
        grid=(B, T // tm),
        in_specs=[
            pl.BlockSpec((1, 1, tm, W), lambda b, t: (b, 0, t, 0)),
            pl.BlockSpec((1, 1, tm, W), lambda b, t: (b, 1, t, 0)),
            pl.BlockSpec((1, tm, RW_COLS), lambda b, t: (b, t, 0)),
            pl.BlockSpec((RW_LG, W), lambda b, t: (0, 0)),
            pl.BlockSpec((1, W), lambda b, t: (0, 0)),
            pl.BlockSpec((1, W), lambda b, t: (0, 0)),
        ],
        out_specs=pl.BlockSpec((1, tm, W), lambda b, t: (b, t, 0)),
        out_shape=jax.ShapeDtypeStruct((B, T, W), F32),
        compiler_params=_cparams("parallel", "parallel"),
        name="rwkv_out",
    )(y2, y2, zrw, g_up, r_k.reshape(1, W), lnx_g.reshape(1, W))


def _softmax_av_steps(scores, values, store):
    m = functools.reduce(jnp.maximum, [jnp.max(s, axis=-1, keepdims=True) for s in scores])
    yield
    ps = [jnp.exp(s - m) for s in scores]
    l = functools.reduce(lambda a, b: a + b, [jnp.sum(p, axis=-1, keepdims=True) for p in ps])
    yield
    o = functools.reduce(lambda a, b: a + b,
                         [jnp.dot(p.astype(BF16), v, preferred_element_type=F32) for p, v in zip(ps, values)])
    yield
    store(o / l)


def _interleave(chains):
    live = list(chains)
    done = object()
    while live:
        live = [g for g in live if next(g, done) is not done]


NA_HEADS_INTERLEAVED = 8


def _na_kernel(q_ref, k_ref, v_ref, kc_ref, vc_ref, bias_ref, o_ref, *, kr):
    r = pl.program_id(1)
    rows = pl.num_programs(1)
    start = pl.multiple_of(jnp.clip(r - NA_KR // 2, 0, rows - kr) * GRID_W, GRID_W)
    nwin = kr * GRID_W

    def head(h):
        sl = slice(h * HEAD_DIM, (h + 1) * HEAD_DIM)
        q = q_ref[0, :, sl]
        kb = k_ref[0, pl.ds(start, nwin), sl]
        vb = v_ref[0, pl.ds(start, nwin), sl]
        s_win = _mm(q, kb, nt=True) + bias_ref[0, h]
        s_ctx = _mm(q, kc_ref[0, :, sl], nt=True)
        yield

        def store(o):
            o_ref[0, :, sl] = o

        yield from _softmax_av_steps([s_win, s_ctx], [vb, vc_ref[0, :, sl]], store)

    for h0 in range(0, NA_H, NA_HEADS_INTERLEAVED):
        _interleave(head(h) for h in range(h0, h0 + NA_HEADS_INTERLEAVED))


def _na_bias_table(rpb, rows, kr):
    ci = np.arange(GRID_W)
    c0 = np.clip(ci - NA_KC // 2, 0, GRID_W - NA_KC)
    col_ok = (ci[None, :] >= c0[:, None]) & (ci[None, :] < c0[:, None] + NA_KC)
    dc = np.clip(ci[None, :] - ci[:, None] + (NA_KC - 1), 0, 2 * NA_KC - 2)
    n_off = NA_KR
    dr = np.arange(kr)[None, :] - np.arange(n_off)[:, None] + (NA_KR - 1)
    dr_ok = (dr >= 0) & (dr < 2 * NA_KR - 1)
    sel_r = (dr[:, :, None] == np.arange(2 * NA_KR - 1)).astype(np.float32)
    sel_c = (dc[:, :, None] == np.arange(2 * NA_KC - 1)).astype(np.float32)
    bias = jnp.einsum('hrc,okr,qwc->ohqkw', rpb.astype(F32), sel_r, sel_c, precision=lax.Precision.HIGHEST)
    ok = dr_ok[:, None, None, :, None] & col_ok[None, None, :, None, :]
    bias = jnp.where(ok, bias, NEG_INF)
    return bias.reshape(n_off, NA_H, GRID_W, kr * GRID_W)


def _na_attention(q, k, v, kc, vc, bias_tab):
    B, S, W = q.shape
    Cn = kc.shape[1]
    rows = S // GRID_W
    kr = min(NA_KR, rows)
    nwin = kr * GRID_W

    def bias_idx(b, r):
        return (r - jnp.clip(r - NA_KR // 2, 0, rows - kr), 0, 0, 0)

    return pl.pallas_call(
        functools.partial(_na_kernel, kr=kr),
        grid=(B, rows),
        in_specs=[
            pl.BlockSpec((1, GRID_W, W), lambda b, r: (b, r, 0)),
            pl.BlockSpec((1, S, W), lambda b, r: (b, 0, 0)),
            pl.BlockSpec((1, S, W), lambda b, r: (b, 0, 0)),
            pl.BlockSpec((1, Cn, W), lambda b, r: (b, 0, 0)),
            pl.BlockSpec((1, Cn, W), lambda b, r: (b, 0, 0)),
            pl.BlockSpec((1, NA_H, GRID_W, nwin), bias_idx),
        ],
        out_specs=pl.BlockSpec((1, GRID_W, W), lambda b, r: (b, r, 0)),
        out_shape=jax.ShapeDtypeStruct((B, S, W), F32),
        compiler_params=_cparams("parallel", "arbitrary"),
        name="na_attn",
    )(q, k, v, kc, vc, bias_tab)


def _ctx_attn_kernel(q_ref, k_ref, v_ref, o_ref):
    def head(h):
        sl = slice(h * HEAD_DIM, (h + 1) * HEAD_DIM)
        s = _mm(q_ref[0, :, sl], k_ref[0, :, sl], nt=True)
        yield

        def store(o):
            o_ref[0, :, sl] = o

        yield from _softmax_av_steps([s], [v_ref[0, :, sl]], store)

    for h0 in range(0, NA_H, NA_HEADS_INTERLEAVED):
        _interleave(head(h) for h in range(h0, h0 + NA_HEADS_INTERLEAVED))


def _ctx_attention(q, k, v):
    B, Cn, W = q.shape
    spec = pl.BlockSpec((1, Cn, W), lambda b: (b, 0, 0))
    return pl.pallas_call(
        _ctx_attn_kernel,
        grid=(B,),
        in_specs=[spec] * 3,
        out_specs=spec,
        out_shape=jax.ShapeDtypeStruct((B, Cn, W), F32),
        compiler_params=_cparams("parallel"),
        name="ctx_attn",
    )(q, k, v)


def _conv_kernel(z_ref, w_ref, o_ref):
    T = z_ref.shape[1]
    z = z_ref[0]
    zb, zc, zx = z[:, :SC_W], z[:, SC_W:2 * SC_W], z[:, 2 * SC_W:]
    u = zc * zx
    t = lax.broadcasted_iota(jnp.int32, u.shape, 0)
    prev = jnp.where(t == 0, 0.0, pltpu.roll(u, 1, 0))
    nxt = jnp.where(t == T - 1, 0.0, pltpu.roll(u, T - 1, 0))
    o_ref[0] = zb * (prev * w_ref[0:1, :] + u * w_ref[1:2, :] + nxt * w_ref[2:3, :])


def _short_conv(zsc, w):
    B, T, _ = zsc.shape
    return pl.pallas_call(
        _conv_kernel,
        grid=(B,),
        in_specs=[pl.BlockSpec((1, T, 3 * SC_W), lambda b: (b, 0, 0)),
                  pl.BlockSpec((3, SC_W), lambda b: (0, 0))],
        out_specs=pl.BlockSpec((1, T, SC_W), lambda b: (b, 0, 0)),
        out_shape=jax.ShapeDtypeStruct((B, T, SC_W), F32),
        compiler_params=_cparams("parallel"),
        name="short_conv",
    )(zsc, w)


def _proj_out_kernel(yrw_ref, yna_ref, ysc_ref, w_ref, x_ref, mod_ref, o_ref):
    a, b = RW_W, RW_W + NA_W
    y = (jnp.dot(yrw_ref[0].astype(BF16), w_ref[:a, :], preferred_element_type=F32)
         + jnp.dot(yna_ref[0].astype(BF16), w_ref[a:b, :], preferred_element_type=F32)
         + jnp.dot(ysc_ref[0].astype(BF16), w_ref[b:, :], preferred_element_type=F32))
    o_ref[0] = x_ref[0] + mod_ref[0, 2] * y


def _proj_out(yrw, yna, ysc, w_bf, x, mod):
    B, T, D = x.shape
    tm = 256
    row = lambda b, t: (b, t, 0)
    return pl.pallas_call(
        _proj_out_kernel,
        grid=(B, T // tm),
        in_specs=[
            pl.BlockSpec((1, tm, RW_W), row),
            pl.BlockSpec((1, tm, NA_W), row),
            pl.BlockSpec((1, tm, SC_W), row),
            pl.BlockSpec(w_bf.shape, lambda b, t: (0, 0)),
            pl.BlockSpec((1, tm, D), row),
            pl.BlockSpec((1, 6, 1, D), lambda b, t: (b, 0, 0, 0)),
        ],
        out_specs=pl.BlockSpec((1, tm, D), row),
        out_shape=jax.ShapeDtypeStruct((B, T, D), F32),
        compiler_params=_cparams("parallel", "parallel"),
        name="proj_out",
    )(yrw, yna, ysc, w_bf, x, mod)


PEER_P_ROUTE = 3


def _peer_query_kernel(x_ref, mod_ref, g_ref, wh_ref, wl_ref, sk_ref, ht_ref, s_ref):
    h = _norm_mod(x_ref[0], g_ref[...], mod_ref[0, 4], mod_ref[0, 3])
    h_hi = h.astype(BF16)
    ht_ref[0] = h.T.astype(BF16)
    q = jnp.dot(h_hi, wh_ref[...], preferred_element_type=F32)
    if PEER_P_ROUTE == 3:
        h_lo = (h - h_hi.astype(F32)).astype(BF16)
        q = q + jnp.dot(h_lo, wh_ref[...], preferred_element_type=F32) \
            + jnp.dot(h_hi, wl_ref[...], preferred_element_type=F32)
    dq = sk_ref.shape[2]
    for hp in range(sk_ref.shape[0]):
        s_ref[0, hp] = _mm(sk_ref[hp], q[:, hp * dq:(hp + 1) * dq], PEER_P_ROUTE, nt=True)


def _peer_query(x, mod, g, qw_hi, qw_lo, sub_keys):
    B, T, D = x.shape
    tm = 256
    nq = qw_hi.shape[1]
    sk = sub_keys.reshape(2 * PEER_H, PEER_NKEYS, -1)
    return pl.pallas_call(
        _peer_query_kernel,
        grid=(B, T // tm),
        in_specs=[
            pl.BlockSpec((1, tm, D), lambda b, t: (b, t, 0)),
            pl.BlockSpec((1, 6, 1, D), lambda b, t: (b, 0, 0, 0)),
            pl.BlockSpec((1, D), lambda b, t: (0, 0)),
            pl.BlockSpec((D, nq), lambda b, t: (0, 0)),
            pl.BlockSpec((D, nq), lambda b, t: (0, 0)),
            pl.BlockSpec(sk.shape, lambda b, t: (0, 0, 0)),
        ],
        out_specs=[pl.BlockSpec((1, D, tm), lambda b, t: (b, 0, t)),
                   pl.BlockSpec((1, 2 * PEER_H, PEER_NKEYS, tm), lambda b, t: (b, 0, 0, t))],
        out_shape=[jax.ShapeDtypeStruct((B, D, T), BF16),
                   jax.ShapeDtypeStruct((B, 2 * PEER_H, PEER_NKEYS, T), F32)],
        compiler_params=_cparams("parallel", "parallel"),
        name="peer_query",
    )(x, mod, g.reshape(1, D), qw_hi, qw_lo, sk)


def _extract_top(s, n, on_pick, stable=True):
    rows = lax.broadcasted_iota(jnp.int32, s.shape, 0).astype(F32)
    big = float(s.shape[0])
    for it in range(n):
        m = jnp.max(s, axis=0, keepdims=True)
        sel = s == m
        if stable:
            idx = jnp.min(jnp.where(sel, rows, big), axis=0, keepdims=True)
            sel = rows == idx
        on_pick(it, m, sel)
        s = jnp.where(sel, -jnp.inf, s)


def _top_ranks(s, rank_scr, val_scr):
    nk = s.shape[0]

    def run(stable):
        rank = [jnp.full(s.shape, float(nk), F32)]

        def pick(it, m, sel):
            val_scr[it:it + 1, :] = m
            rank[0] = jnp.where(sel, float(it), rank[0])

        _extract_top(s, PEER_TOPK, pick, stable=stable)
        rank_scr[...] = rank[0]
        return rank[0]

    rank = run(stable=False)
    n_ranked = jnp.sum(jnp.where(rank < float(PEER_TOPK), 1.0, 0.0), axis=0, keepdims=True)
    over = jnp.broadcast_to(jnp.where(n_ranked > float(PEER_TOPK), 1.0, 0.0), (8, s.shape[1]))

    @pl.when(jnp.max(over) > 0.0)
    def _():
        run(stable=True)


def _dup_bf16(x):
    bits = lax.bitcast_convert_type(x.astype(BF16).astype(F32), jnp.uint32)
    return bits | (bits >> 16)


def _peer_route_kernel(s_ref, rank2_ref, cnt_ref, e1_ref, e2_ref, v1_scr, v2_scr, cand_scr, r1_scr, r2_scr):
    tn = s_ref.shape[3]
    for h in range(PEER_H):
        s1 = s_ref[0, 2 * h]
        s2 = s_ref[0, 2 * h + 1]
        _top_ranks(s1, r1_scr, v1_scr)
        _top_ranks(s2, r2_scr, v2_scr)
        rank1, rank2 = r1_scr[...], r2_scr[...]
        cand_scr[...] = jnp.full(cand_scr.shape, -jnp.inf, F32)
        for ci, (a, b) in enumerate(_CAND):
            cand_scr[ci:ci + 1, :] = v1_scr[a:a + 1, :] + v2_scr[b:b + 1, :]
        picked = [jnp.zeros(cand_scr.shape, F32), jnp.zeros((1, tn), F32), None]

        def pick_c(it, m, sel, picked=picked):
            if it == 0:
                picked[2] = m
            picked[0] = jnp.where(sel, 1.0, picked[0])
            picked[1] = picked[1] + jnp.exp(m - picked[2])

        _extract_top(cand_scr[...], PEER_TOPK, pick_c)
        selmask, z = picked[0], picked[1]
        ca = lax.broadcasted_iota(jnp.int32, (PEER_TOPK, _NCAND_PAD), 0)
        cj = lax.broadcasted_iota(jnp.int32, (PEER_TOPK, _NCAND_PAD), 1)
        row_of = functools.reduce(jnp.logical_or, [(ca == a) & (cj == ci) for ci, (a, _) in enumerate(_CAND)])
        cnt_a = jnp.dot(row_of.astype(BF16), selmask.astype(BF16), preferred_element_type=F32)
        cnt = jnp.zeros(s1.shape, F32)
        for a in range(PEER_TOPK):
            cnt = jnp.where(rank1 == float(a), cnt_a[a:a + 1, :], cnt)
        rank2_ref[0, h] = rank2.astype(BF16)
        cnt_ref[0, h] = _dup_bf16(cnt)
        e1_ref[0, h] = _dup_bf16(jnp.exp(s1 - v1_scr[0:1, :]) / z)
        e2_ref[0, h] = jnp.exp(s2 - v2_scr[0:1, :]).astype(BF16)


def _peer_route(sT):
    B, _, NK, T = sT.shape
    tn = 256
    ospec = pl.BlockSpec((1, PEER_H, NK, tn), lambda b, t: (b, 0, 0, t))
    oshape = lambda dt: jax.ShapeDtypeStruct((B, PEER_H, NK, T), dt)
    return pl.pallas_call(
        _peer_route_kernel,
        grid=(B, T // tn),
        in_specs=[pl.BlockSpec((1, 2 * PEER_H, NK, tn), lambda b, t: (b, 0, 0, t))],
        out_specs=[ospec] * 4,
        out_shape=[oshape(BF16), oshape(jnp.uint32), oshape(jnp.uint32), oshape(BF16)],
        scratch_shapes=[pltpu.VMEM((PEER_TOPK, tn), F32), pltpu.VMEM((PEER_TOPK, tn), F32),
                        pltpu.VMEM((_NCAND_PAD, tn), F32),
                        pltpu.VMEM((NK, tn), F32), pltpu.VMEM((NK, tn), F32)],
        compiler_params=_cparams("parallel", "parallel"),
        name="peer_route",
    )(sT)


_GELU_C = float(np.sqrt(2.0 / np.pi))
PEER_SUB = 256


def _gelu_tanh(x):
    hx = 0.5 * x
    return hx + hx * jnp.tanh(x * (_GELU_C + (_GELU_C * 0.044715) * (x * x)))


def _row_tile(words):
    return pltpu.bitcast(jnp.broadcast_to(words, (PEER_NKEYS // 2, words.shape[1])), BF16)


def _peer_expert_kernel(ht_ref, rank2_ref, cnt_ref, e1_ref, e2_ref, u_ref, vt_ref, x_ref, mod_ref,
                        fg_ref, o_ref, acc_ref, g_scr, *, te, final_norm):
    e = pl.program_id(2)

    @pl.when(e == 0)
    def _():
        acc_ref[...] = jnp.zeros(acc_ref.shape, F32)

    ht = ht_ref[0]
    rows = te // PEER_NKEYS
    for ii in range(rows):
        i = e * rows + ii
        g = None
        for hd in range(PEER_H):
            cnt_row = _row_tile(cnt_ref[0, hd, pl.ds(i, 1), :])
            e1_row = _row_tile(e1_ref[0, hd, pl.ds(i, 1), :])
            gh = jnp.where(rank2_ref[0, hd] < cnt_row, e2_ref[0, hd] * e1_row, jnp.zeros((), BF16))
            g = gh if g is None else g + gh
        g_scr[ii * PEER_NKEYS:(ii + 1) * PEER_NKEYS, :] = g
    for sb in range(te // PEER_SUB):
        es = slice(sb * PEER_SUB, (sb + 1) * PEER_SUB)
        a = jnp.dot(u_ref[es, :], ht, preferred_element_type=F32)
        g_scr[es, :] = _gelu_tanh(a.astype(BF16)) * g_scr[es, :]
    acc_ref[...] += jnp.dot(vt_ref[...], g_scr[...], preferred_element_type=F32)

    @pl.when(e == pl.num_programs(2) - 1)
    def _():
        y = x_ref[0] + mod_ref[0, 5] * acc_ref[...].T
        if final_norm:
            y = y * lax.rsqrt(jnp.mean(y * y, axis=-1, keepdims=True) + NORM_EPS) * fg_ref[...]
        o_ref[0] = y


def _peer_expert(h2, route, u_bf, vt_bf, x, mod, final_g, final_norm):
    B, T, D = x.shape
    tn = min(512, T)
    te = 2048
    ne = u_bf.shape[0] // te
    rspec = pl.BlockSpec((1, PEER_H, PEER_NKEYS, tn), lambda b, t, e: (b, 0, 0, t))
    return pl.pallas_call(
        functools.partial(_peer_expert_kernel, te=te, final_norm=final_norm),
        grid=(B, T // tn, ne),
        in_specs=[
            pl.BlockSpec((1, D, tn), lambda b, t, e: (b, 0, t)),
            rspec, rspec, rspec, rspec,
            pl.BlockSpec((te, D), lambda b, t, e: (e, 0)),
            pl.BlockSpec((D, te), lambda b, t, e: (0, e)),
            pl.BlockSpec((1, tn, D), lambda b, t, e: (b, t, 0)),
            pl.BlockSpec((1, 6, 1, D), lambda b, t, e: (b, 0, 0, 0)),
            pl.BlockSpec((1, D), lambda b, t, e: (0, 0)),
        ],
        out_specs=pl.BlockSpec((1, tn, D), lambda b, t, e: (b, t, 0)),
        out_shape=jax.ShapeDtypeStruct((B, T, D), F32),
        scratch_shapes=[pltpu.VMEM((D, tn), F32), pltpu.VMEM((te, tn), BF16)],
        compiler_params=_cparams("parallel", "parallel", "arbitrary"),
        name="peer_expert",
    )(h2, *route, u_bf, vt_bf, x, mod, final_g.reshape(1, D))


def _peer(x, mod, g, qw_hi, qw_lo, sub_keys, u_bf, vt_bf, final_g, final_norm):
    h2, sT = _peer_query(x, mod, g, qw_hi, qw_lo, sub_keys)
    route = _peer_route(sT)
    return _peer_expert(h2, route, u_bf, vt_bf, x, mod, final_g, final_norm)


def _split_bf16(w):
    hi = w.astype(BF16)
    return hi, (w - hi.astype(F32)).astype(BF16)


def kernel(x, c, ctx, c_ctx, ada_w, ada_b, norm1_g, norm2_g, w_in, rw_w0, rw_w_up, rw_a0, rw_a_up, rw_g_up,
           rw_k_k, rw_k_a, rw_r_k, rw_lnx_g, na_rpb, sc_conv_w, w_out, peer_q_w, peer_sub_keys, peer_u,
           peer_v, final_g):
    B, S, D = x.shape
    depth = ada_w.shape[0]
    rows = S // GRID_W
    kr = min(NA_KR, rows)
    n_mod = -(-(B + 1) // 8) * 8
    cc = jnp.concatenate([c, c_ctx[None, :], jnp.zeros((n_mod - B - 1, D), F32)], axis=0)
    mod = _ada(cc, ada_w, ada_b)
    xl, xc = x, ctx
    for i in range(depth):
        need_ctx = i < depth - 1
        mod_l = mod[i, :B].reshape(B, 6, 1, D)
        mod_c = jnp.broadcast_to(mod[i, B].reshape(1, 6, 1, D), (B, 6, 1, D))
        w_in_bf = w_in[i].astype(BF16)
        w_out_bf = w_out[i].astype(BF16)
        rw = (rw_w0[i], rw_w_up[i], rw_a0[i], rw_a_up[i], rw_k_k[i], rw_k_a[i])

        zrw_l, q_l, k_l, v_l, zsc_l = _proj_in(xl, mod_l, norm1_g[i], w_in_bf)
        zrw_c, q_c, k_c, v_c, zsc_c = _proj_in(xc, mod_c, norm1_g[i], w_in_bf)
        y2_c, y2_l = _rwkv_state(_rwkv_prep(zrw_c, *rw), _rwkv_prep(zrw_l, *rw))
        yrw_l = _rwkv_out(y2_l, zrw_l, rw_g_up[i], rw_r_k[i], rw_lnx_g[i])
        yna_l = _na_attention(q_l, k_l, v_l, k_c, v_c, _na_bias_table(na_rpb[i], rows, kr))
        ysc_l = _short_conv(zsc_l, sc_conv_w[i])
        xl = _proj_out(yrw_l, yna_l, ysc_l, w_out_bf, xl, mod_l)

        qw_hi, qw_lo = _split_bf16(peer_q_w[i])
        u_bf = peer_u[i].astype(BF16)
        vt_bf = peer_v[i].astype(BF16).T
        peer_args = (norm2_g[i], qw_hi, qw_lo, peer_sub_keys[i], u_bf, vt_bf, final_g)
        xl = _peer(xl, mod_l, *peer_args, final_norm=not need_ctx)
        if need_ctx:
            yrw_c = _rwkv_out(y2_c, zrw_c, rw_g_up[i], rw_r_k[i], rw_lnx_g[i])
            yna_c = _ctx_attention(q_c, k_c, v_c)
            ysc_c = _short_conv(zsc_c, sc_conv_w[i])
            xc = _proj_out(yrw_c, yna_c, ysc_c, w_out_bf, xc, mod_c)
            xc = _peer(xc, mod_c, *peer_args, final_norm=False)
    return xl
```

```python
import functools

import numpy as np
import jax
import jax.numpy as jnp
from jax import lax
from jax.experimental import pallas as pl
from jax.experimental.pallas import tpu as pltpu

F32 = jnp.float32
BF16 = jnp.bfloat16

HEAD_DIM = 64
GRID_W = 64
RW_H = 4
RW_W = RW_H * HEAD_DIM
RW_LW = 64
RW_LA = 64
RW_LG = 128
RW_COLS = 3 * RW_W + 2 * RW_LW + 2 * RW_LA + RW_LG
NA_H = 8
NA_W = NA_H * HEAD_DIM
NA_KR = 8
NA_KC = 16
SC_W = 256
PEER_H = 8
PEER_NKEYS = 128
PEER_TOPK = 16
NORM_EPS = 1e-6
GN_EPS = 64e-5
NEG_INF = -1e30
RW_CHUNK = 64

VMEM_LIMIT = 52 * 1024 * 1024

_CAND = [(a, b) for a in range(PEER_TOPK) for b in range(PEER_TOPK) if (a + 1) * (b + 1) <= PEER_TOPK]
_NCAND = len(_CAND)
_NCAND_PAD = -(-_NCAND // 8) * 8


def _cparams(*sem):
    return pltpu.CompilerParams(dimension_semantics=sem, vmem_limit_bytes=VMEM_LIMIT)


def _mm(a, b, passes=1, nt=False):
    dims = (((1,), (1,)), ((), ())) if nt else (((1,), (0,)), ((), ()))
    dg = functools.partial(lax.dot_general, dimension_numbers=dims, preferred_element_type=F32)
    if passes == 6:
        return dg(a.astype(F32), b.astype(F32), precision=lax.Precision.HIGHEST)
    a_hi = a.astype(BF16)
    b_hi = b.astype(BF16)
    out = dg(a_hi, b_hi)
    if passes == 3:
        a_lo = (a - a_hi.astype(F32)).astype(BF16)
        b_lo = (b - b_hi.astype(F32)).astype(BF16)
        out = out + dg(a_lo, b_hi) + dg(a_hi, b_lo)
    return out


def _sigmoid(x):
    return 1.0 / (1.0 + jnp.exp(-x))


def _norm_mod(x, g, scale, shift):
    ms = jnp.mean(x * x, axis=-1, keepdims=True)
    y = x * lax.rsqrt(ms + NORM_EPS) * g
    return y * (1.0 + scale) + shift


def _ada_kernel(c_ref, w_ref, b_ref, o_ref):
    cc = c_ref[...]
    o_ref[0] = _mm(cc * _sigmoid(cc), w_ref[0], 6) + b_ref[0]


def _ada(cc, ada_w, ada_b):
    L, D, D6 = ada_w.shape
    R = cc.shape[0]
    tn = 1536
    return pl.pallas_call(
        _ada_kernel,
        grid=(L, D6 // tn),
        in_specs=[
            pl.BlockSpec((R, D), lambda l, j: (0, 0)),
            pl.BlockSpec((1, D, tn), lambda l, j: (l, 0, j)),
            pl.BlockSpec((1, 1, tn), lambda l, j: (l, 0, j)),
        ],
        out_specs=pl.BlockSpec((1, R, tn), lambda l, j: (l, 0, j)),
        out_shape=jax.ShapeDtypeStruct((L, R, D6), F32),
        compiler_params=_cparams("parallel", "parallel"),
        name="ada",
    )(cc, ada_w, ada_b.reshape(L, 1, D6))


def _proj_in_kernel(x_ref, mod_ref, g_ref, w_ref, zrw_ref, q_ref, k_ref, v_ref, zsc_ref):
    h = _norm_mod(x_ref[0], g_ref[...], mod_ref[0, 1], mod_ref[0, 0])
    z = jnp.dot(h.astype(BF16), w_ref[...], preferred_element_type=F32)
    o = RW_COLS
    zrw_ref[0] = z[:, :o]
    q_ref[0] = (z[:, o:o + NA_W] * (HEAD_DIM ** -0.5)).astype(BF16)
    k_ref[0] = z[:, o + NA_W:o + 2 * NA_W].astype(BF16)
    v_ref[0] = z[:, o + 2 * NA_W:o + 3 * NA_W].astype(BF16)
    zsc_ref[0] = z[:, o + 3 * NA_W:]


def _proj_in(x, mod, g, w_bf):
    B, T, D = x.shape
    tm = 256
    n_out = w_bf.shape[1]
    row = lambda b, t: (b, t, 0)
    return pl.pallas_call(
        _proj_in_kernel,
        grid=(B, T // tm),
        in_specs=[
            pl.BlockSpec((1, tm, D), row),
            pl.BlockSpec((1, 6, 1, D), lambda b, t: (b, 0, 0, 0)),
            pl.BlockSpec((1, D), lambda b, t: (0, 0)),
            pl.BlockSpec((D, n_out), lambda b, t: (0, 0)),
        ],
        out_specs=[
            pl.BlockSpec((1, tm, RW_COLS), row),
            pl.BlockSpec((1, tm, NA_W), row),
            pl.BlockSpec((1, tm, NA_W), row),
            pl.BlockSpec((1, tm, NA_W), row),
            pl.BlockSpec((1, tm, 3 * SC_W), row),
        ],
        out_shape=[
            jax.ShapeDtypeStruct((B, T, RW_COLS), F32),
            jax.ShapeDtypeStruct((B, T, NA_W), BF16),
            jax.ShapeDtypeStruct((B, T, NA_W), BF16),
            jax.ShapeDtypeStruct((B, T, NA_W), BF16),
            jax.ShapeDtypeStruct((B, T, 3 * SC_W), F32),
        ],
        compiler_params=_cparams("parallel", "parallel"),
        name="proj_in",
    )(x, mod, g.reshape(1, D), w_bf)


RW_CHUNKS_PER_STEP = 2
RW_P_ALG = 1
RW_P_LORA = 3


def _rwkv_prep_kernel(z_ref, w0_ref, wup_ref, a0_ref, aup_ref, kk_ref, ka_ref,
                      phi_ref, zc_ref, wr_ref, yin_ref):
    C = RW_CHUNK
    W = RW_W
    assert C == HEAD_DIM
    N = RW_H * C
    row = lax.broadcasted_iota(jnp.int32, (C, C), 0)
    col = lax.broadcasted_iota(jnp.int32, (C, C), 1)
    brow = lax.broadcasted_iota(jnp.int32, (N, W), 0)
    bcol = lax.broadcasted_iota(jnp.int32, (N, W), 1)
    bd = (brow // C) == (bcol // HEAD_DIM)
    eye_j = lax.broadcasted_iota(jnp.int32, (HEAD_DIM, W), 0) == \
        lax.broadcasted_iota(jnp.int32, (HEAD_DIM, W), 1) % HEAD_DIM
    head_ones = bd.astype(F32)
    mm = functools.partial(_mm, passes=RW_P_ALG)
    mm_tn = lambda a, b: lax.dot_general(a.astype(BF16), b.astype(BF16), (((0,), (0,)), ((), ())),
                                         preferred_element_type=F32)
    collapse = lambda m: functools.reduce(lambda a, b: a + b, [m[h * C:(h + 1) * C] for h in range(RW_H)])

    def direction(ci, d):
        z = z_ref[0, ci * C:(ci + 1) * C, :]
        r = z[:, 0:W]
        k = z[:, W:2 * W]
        v = z[:, 2 * W:3 * W]
        zw = z[:, 3 * W:3 * W + 2 * RW_LW]
        za = z[:, 3 * W + 2 * RW_LW:3 * W + 2 * RW_LW + 2 * RW_LA]
        kk = k * kk_ref[...]
        kk = kk * lax.rsqrt(_mm(kk * kk, head_ones, 3) + 1e-12)
        lw = jnp.tanh(zw[:, d * RW_LW:(d + 1) * RW_LW])
        la = za[:, d * RW_LA:(d + 1) * RW_LA]
        w_raw = w0_ref[d:d + 1, :] + _mm(lw, wup_ref[d], RW_P_LORA)
        softplus = jnp.maximum(-w_raw, 0.0) + jnp.log(1.0 + jnp.exp(-jnp.abs(w_raw)))
        logw = -jnp.exp(-softplus - 0.5)
        a = _sigmoid(a0_ref[d:d + 1, :] + _mm(la, aup_ref[d], RW_P_LORA))
        k_d = k * (1.0 + (a - 1.0) * ka_ref[...])
        b_d = kk * a
        if d == 0:
            tri_incl, tri_strict = col <= row, col < row
        else:
            tri_incl, tri_strict = col >= row, col > row
        cum = _mm(tri_incl.astype(F32), logw, 6)
        tot = jnp.sum(logw, axis=0, keepdims=True)
        e_neg = jnp.exp(-cum)
        e_end = jnp.exp(tot - cum)
        a_t = -kk * jnp.exp(cum - logw)
        r_t = r * jnp.exp(cum)
        b_t = b_d * e_neg
        k_t = k_d * e_neg
        b_p = b_d * e_end
        k_p = k_d * e_end
        p_c = jnp.exp(tot)
        bdx = lambda x: jnp.where(bd, jnp.concatenate([x] * RW_H, axis=0), 0.0)
        a_bd, r_bd, v_bd = bdx(a_t), bdx(r_t), bdx(v)
        yield
        p4 = mm(jnp.concatenate([a_bd, r_bd], axis=0),
                jnp.concatenate([bdx(b_t), bdx(k_t)], axis=0), nt=True)
        yield
        tri_strict_bd = bd & ((bcol < brow) if d == 0 else (bcol > brow))
        tri_incl_bd = bd & ((bcol <= brow) if d == 0 else (bcol >= brow))
        l_ab = jnp.where(tri_strict_bd, p4[:N, :N], 0.0)
        l_ak = jnp.where(tri_strict_bd, p4[:N, N:], 0.0)
        m_rb = jnp.where(tri_incl_bd, p4[N:, :N], 0.0)
        m_rk = jnp.where(tri_incl_bd, p4[N:, N:], 0.0)
        t_inv = jnp.where(brow == bcol, 1.0, l_ab)
        pw = l_ab
        n = 2
        lakv = mm(l_ak, v_bd)
        mrkv = mm(m_rk, v_bd)
        kpv = mm_tn(bdx(k_p), v_bd)
        while n < C:
            pw = mm(pw, pw)
            yield
            t_inv = t_inv + mm(t_inv, pw)
            n *= 2
        yield
        x = mm(t_inv, jnp.concatenate([a_bd, lakv], axis=1))
        yield
        wy = mm(m_rb, x)
        pz = mm_tn(bdx(b_p), x)
        wr_ref[0, ci, d] = r_t + collapse(wy[:, :W])
        yin_ref[0, ci, d] = collapse(wy[:, W:] + mrkv)
        phi_ref[0, ci, d] = jnp.where(eye_j, p_c, 0.0) + collapse(pz[:, :W])
        zc_ref[0, ci, d] = collapse(pz[:, W:] + kpv)

    live = [direction(ci, d) for ci in range(z_ref.shape[1] // C) for d in range(2)]
    done = object()
    while live:
        live = [g for g in live if next(g, done) is not done]


def _rwkv_prep(zrw, w0, w_up, a0, a_up, k_k, k_a):
    B, T, _ = zrw.shape
    C = RW_CHUNK
    nch = T // C
    full = lambda a: pl.BlockSpec(a.shape, lambda b, c: (0,) * a.ndim)
    k_k = k_k.reshape(1, RW_W)
    k_a = k_a.reshape(1, RW_W)
    cps = RW_CHUNKS_PER_STEP
    ospec = pl.BlockSpec((1, cps, 2, HEAD_DIM, RW_W), lambda b, c: (b, c, 0, 0, 0))
    oshape = jax.ShapeDtypeStruct((B, nch, 2, HEAD_DIM, RW_W), F32)
    return pl.pallas_call(
        _rwkv_prep_kernel,
        grid=(B, nch // cps),
        in_specs=[pl.BlockSpec((1, cps * C, RW_COLS), lambda b, c: (b, c, 0)),
                  full(w0), full(w_up), full(a0), full(a_up), full(k_k), full(k_a)],
        out_specs=[ospec] * 4,
        out_shape=[oshape] * 4,
        compiler_params=_cparams("parallel", "parallel"),
        name="rwkv_prep",
    )(zrw, w0, w_up, a0, a_up, k_k, k_a)


RW_P_STATE = 3


def _rwkv_state_kernel(phic, zcc, wrc, yinc, phil, zcl, wrl, yinl, yc_ref, yl_ref):
    mm = functools.partial(_mm, passes=RW_P_STATE)

    N = RW_H * HEAD_DIM
    bd = (lax.broadcasted_iota(jnp.int32, (N, RW_W), 0) // HEAD_DIM) == \
        (lax.broadcasted_iota(jnp.int32, (N, RW_W), 1) // HEAD_DIM)

    def run(Hs, phi, zc, wr, yin, y_ref):
        nch = phi.shape[1]

        def body(s, Hs):
            new = []
            for d, H in enumerate(Hs):
                c = s if d == 0 else nch - 1 - s
                t0 = pl.multiple_of(c * RW_CHUNK, RW_CHUNK)
                h_bd = jnp.where(bd, jnp.concatenate([H] * RW_H, axis=0), 0.0)
                ops = jnp.concatenate([wr[0, c, d], phi[0, c, d]], axis=0)
                out = mm(ops, h_bd) + jnp.concatenate([yin[0, c, d], zc[0, c, d]], axis=0)
                y_ref[0, d, pl.ds(t0, RW_CHUNK), :] = out[:RW_CHUNK]
                new.append(out[RW_CHUNK:])
            return tuple(new)

        return lax.fori_loop(0, nch, body, Hs)

    Hs = (jnp.zeros((HEAD_DIM, RW_W), F32),) * 2
    Hs = run(Hs, phic, zcc, wrc, yinc, yc_ref)
    run(Hs, phil, zcl, wrl, yinl, yl_ref)


def _rwkv_state(ops_c, ops_l):
    B, nc = ops_c[0].shape[:2]
    nl = ops_l[0].shape[1]
    spec = lambda n: pl.BlockSpec((1, n, 2, HEAD_DIM, RW_W), lambda b: (b, 0, 0, 0, 0))
    yspec = lambda n: pl.BlockSpec((1, 2, n * RW_CHUNK, RW_W), lambda b: (b, 0, 0, 0))
    return pl.pallas_call(
        _rwkv_state_kernel,
        grid=(B,),
        in_specs=[spec(nc)] * 4 + [spec(nl)] * 4,
        out_specs=[yspec(nc), yspec(nl)],
        out_shape=[jax.ShapeDtypeStruct((B, 2, nc * RW_CHUNK, RW_W), F32),
                   jax.ShapeDtypeStruct((B, 2, nl * RW_CHUNK, RW_W), F32)],
        compiler_params=_cparams("parallel"),
        name="rwkv_state",
    )(*ops_c, *ops_l)


def _rwkv_out_kernel(yf_ref, yr_ref, z_ref, gup_ref, rk_ref, lnx_ref, o_ref):
    W = RW_W
    y = yf_ref[0, 0] + yr_ref[0, 0]
    z = z_ref[0]
    r, k, v = z[:, 0:W], z[:, W:2 * W], z[:, 2 * W:3 * W]
    zg = z[:, RW_COLS - RW_LG:RW_COLS]
    hrow = lax.broadcasted_iota(jnp.int32, (W, W), 0) // HEAD_DIM
    hcol = lax.broadcasted_iota(jnp.int32, (W, W), 1) // HEAD_DIM
    head_ones = (hrow == hcol).astype(F32)
    inv = 1.0 / HEAD_DIM
    mu = _mm(y, head_ones, 3) * inv
    yc = y - mu
    var = _mm(yc * yc, head_ones, 3) * inv
    yn = yc * lax.rsqrt(var + GN_EPS) * lnx_ref[...]
    bonus = _mm(r * k * rk_ref[...], head_ones, 3)
    yn = yn + bonus * v
    gate = _mm(_sigmoid(zg), gup_ref[...], 3)
    o_ref[0] = yn * gate


def _rwkv_out(y2, zrw, g_up, r_k, lnx_g):
    B, _, T, W = y2.shape
    tm = 256
    return pl.pallas_call(
        _rwkv_out_kernel,
        grid=(B, T // tm),
        in_specs=[
            pl.BlockSpec((1, 1, tm, W), lambda b, t: (b, 0, t, 0)),
            pl.BlockSpec((1, 1, tm, W), lambda b, t: (b, 1, t, 0)),
            pl.BlockSpec((1, tm, RW_COLS), lambda b, t: (b, t, 0)),
            pl.BlockSpec((RW_LG, W), lambda b, t: (0, 0)),
            pl.BlockSpec((1, W), lambda b, t: (0, 0)),
            pl.BlockSpec((1, W), lambda b, t: (0, 0)),
        ],
        out_specs=pl.BlockSpec((1, tm, W), lambda b, t: (b, t, 0)),
        out_shape=jax.ShapeDtypeStruct((B, T, W), F32),
        compiler_params=_cparams("parallel", "parallel"),
        name="rwkv_out",
    )(y2, y2, zrw, g_up, r_k.reshape(1, W), lnx_g.reshape(1, W))


def _softmax_av_steps(scores, values, store):
    m = functools.reduce(jnp.maximum, [jnp.max(s, axis=-1, keepdims=True) for s in scores])
    yield
    ps = [jnp.exp(s - m) for s in scores]
    l = functools.reduce(lambda a, b: a + b, [jnp.sum(p, axis=-1, keepdims=True) for p in ps])
    yield
    o = functools.reduce(lambda a, b: a + b,
                         [jnp.dot(p.astype(BF16), v, preferred_element_type=F32) for p, v in zip(ps, values)])
    yield
    store(o / l)


def _interleave(chains):
    live = list(chains)
    done = object()
    while live:
        live = [g for g in live if next(g, done) is not done]


NA_HEADS_INTERLEAVED = 8


def _na_kernel(q_ref, k_ref, v_ref, kc_ref, vc_ref, bias_ref, o_ref, *, kr):
    r = pl.program_id(1)
    rows = pl.num_programs(1)
    start = pl.multiple_of(jnp.clip(r - NA_KR // 2, 0, rows - kr) * GRID_W, GRID_W)
    nwin = kr * GRID_W

    def head(h):
        sl = slice(h * HEAD_DIM, (h + 1) * HEAD_DIM)
        q = q_ref[0, :, sl]
        kb = k_ref[0, pl.ds(start, nwin), sl]
        vb = v_ref[0, pl.ds(start, nwin), sl]
        s_win = _mm(q, kb, nt=True) + bias_ref[0, h]
        s_ctx = _mm(q, kc_ref[0, :, sl], nt=True)
        yield

        def store(o):
            o_ref[0, :, sl] = o

        yield from _softmax_av_steps([s_win, s_ctx], [vb, vc_ref[0, :, sl]], store)

    for h0 in range(0, NA_H, NA_HEADS_INTERLEAVED):
        _interleave(head(h) for h in range(h0, h0 + NA_HEADS_INTERLEAVED))


def _na_bias_table(rpb, rows, kr):
    ci = np.arange(GRID_W)
    c0 = np.clip(ci - NA_KC // 2, 0, GRID_W - NA_KC)
    col_ok = (ci[None, :] >= c0[:, None]) & (ci[None, :] < c0[:, None] + NA_KC)
    dc = np.clip(ci[None, :] - ci[:, None] + (NA_KC - 1), 0, 2 * NA_KC - 2)
    n_off = NA_KR
    dr = np.arange(kr)[None, :] - np.arange(n_off)[:, None] + (NA_KR - 1)
    dr_ok = (dr >= 0) & (dr < 2 * NA_KR - 1)
    sel_r = (dr[:, :, None] == np.arange(2 * NA_KR - 1)).astype(np.float32)
    sel_c = (dc[:, :, None] == np.arange(2 * NA_KC - 1)).astype(np.float32)
    bias = jnp.einsum('hrc,okr,qwc->ohqkw', rpb.astype(F32), sel_r, sel_c, precision=lax.Precision.HIGHEST)
    ok = dr_ok[:, None, None, :, None] & col_ok[None, None, :, None, :]
    bias = jnp.where(ok, bias, NEG_INF)
    return bias.reshape(n_off, NA_H, GRID_W, kr * GRID_W)


def _na_attention(q, k, v, kc, vc, bias_tab):
    B, S, W = q.shape
    Cn = kc.shape[1]
    rows = S // GRID_W
    kr = min(NA_KR, rows)
    nwin = kr * GRID_W

    def bias_idx(b, r):
        return (r - jnp.clip(r - NA_KR // 2, 0, rows - kr), 0, 0, 0)

    return pl.pallas_call(
        functools.partial(_na_kernel, kr=kr),
        grid=(B, rows),
        in_specs=[
            pl.BlockSpec((1, GRID_W, W), lambda b, r: (b, r, 0)),
            pl.BlockSpec((1, S, W), lambda b, r: (b, 0, 0)),
            pl.BlockSpec((1, S, W), lambda b, r: (b, 0, 0)),
            pl.BlockSpec((1, Cn, W), lambda b, r: (b, 0, 0)),
            pl.BlockSpec((1, Cn, W), lambda b, r: (b, 0, 0)),
            pl.BlockSpec((1, NA_H, GRID_W, nwin), bias_idx),
        ],
        out_specs=pl.BlockSpec((1, GRID_W, W), lambda b, r: (b, r, 0)),
        out_shape=jax.ShapeDtypeStruct((B, S, W), F32),
        compiler_params=_cparams("parallel", "arbitrary"),
        name="na_attn",
    )(q, k, v, kc, vc, bias_tab)


def _ctx_attn_kernel(q_ref, k_ref, v_ref, o_ref):
    def head(h):
        sl = slice(h * HEAD_DIM, (h + 1) * HEAD_DIM)
        s = _mm(q_ref[0, :, sl], k_ref[0, :, sl], nt=True)
        yield

        def store(o):
            o_ref[0, :, sl] = o

        yield from _softmax_av_steps([s], [v_ref[0, :, sl]], store)

    for h0 in range(0, NA_H, NA_HEADS_INTERLEAVED):
        _interleave(head(h) for h in range(h0, h0 + NA_HEADS_INTERLEAVED))


def _ctx_attention(q, k, v):
    B, Cn, W = q.shape
    spec = pl.BlockSpec((1, Cn, W), lambda b: (b, 0, 0))
    return pl.pallas_call(
        _ctx_attn_kernel,
        grid=(B,),
        in_specs=[spec] * 3,
        out_specs=spec,
        out_shape=jax.ShapeDtypeStruct((B, Cn, W), F32),
        compiler_params=_cparams("parallel"),
        name="ctx_attn",
    )(q, k, v)


def _conv_kernel(z_ref, w_ref, o_ref):
    T = z_ref.shape[1]
    z = z_ref[0]
    zb, zc, zx = z[:, :SC_W], z[:, SC_W:2 * SC_W], z[:, 2 * SC_W:]
    u = zc * zx
    t = lax.broadcasted_iota(jnp.int32, u.shape, 0)
    prev = jnp.where(t == 0, 0.0, pltpu.roll(u, 1, 0))
    nxt = jnp.where(t == T - 1, 0.0, pltpu.roll(u, T - 1, 0))
    o_ref[0] = zb * (prev * w_ref[0:1, :] + u * w_ref[1:2, :] + nxt * w_ref[2:3, :])


def _short_conv(zsc, w):
    B, T, _ = zsc.shape
    return pl.pallas_call(
        _conv_kernel,
        grid=(B,),
        in_specs=[pl.BlockSpec((1, T, 3 * SC_W), lambda b: (b, 0, 0)),
                  pl.BlockSpec((3, SC_W), lambda b: (0, 0))],
        out_specs=pl.BlockSpec((1, T, SC_W), lambda b: (b, 0, 0)),
        out_shape=jax.ShapeDtypeStruct((B, T, SC_W), F32),
        compiler_params=_cparams("parallel"),
        name="short_conv",
    )(zsc, w)


def _proj_out_kernel(yrw_ref, yna_ref, ysc_ref, w_ref, x_ref, mod_ref, o_ref):
    a, b = RW_W, RW_W + NA_W
    y = (jnp.dot(yrw_ref[0].astype(BF16), w_ref[:a, :], preferred_element_type=F32)
         + jnp.dot(yna_ref[0].astype(BF16), w_ref[a:b, :], preferred_element_type=F32)
         + jnp.dot(ysc_ref[0].astype(BF16), w_ref[b:, :], preferred_element_type=F32))
    o_ref[0] = x_ref[0] + mod_ref[0, 2] * y


def _proj_out(yrw, yna, ysc, w_bf, x, mod):
    B, T, D = x.shape
    tm = 256
    row = lambda b, t: (b, t, 0)
    return pl.pallas_call(
        _proj_out_kernel,
        grid=(B, T // tm),
        in_specs=[
            pl.BlockSpec((1, tm, RW_W), row),
            pl.BlockSpec((1, tm, NA_W), row),
            pl.BlockSpec((1, tm, SC_W), row),
            pl.BlockSpec(w_bf.shape, lambda b, t: (0, 0)),
            pl.BlockSpec((1, tm, D), row),
            pl.BlockSpec((1, 6, 1, D), lambda b, t: (b, 0, 0, 0)),
        ],
        out_specs=pl.BlockSpec((1, tm, D), row),
        out_shape=jax.ShapeDtypeStruct((B, T, D), F32),
        compiler_params=_cparams("parallel", "parallel"),
        name="proj_out",
    )(yrw, yna, ysc, w_bf, x, mod)


PEER_P_ROUTE = 3


def _peer_query_kernel(x_ref, mod_ref, g_ref, wh_ref, wl_ref, sk_ref, ht_ref, s_ref):
    h = _norm_mod(x_ref[0], g_ref[...], mod_ref[0, 4], mod_ref[0, 3])
    h_hi = h.astype(BF16)
    ht_ref[0] = h.T.astype(BF16)
    q = jnp.dot(h_hi, wh_ref[...], preferred_element_type=F32)
    if PEER_P_ROUTE == 3:
        h_lo = (h - h_hi.astype(F32)).astype(BF16)
        q = q + jnp.dot(h_lo, wh_ref[...], preferred_element_type=F32) \
            + jnp.dot(h_hi, wl_ref[...], preferred_element_type=F32)
    dq = sk_ref.shape[2]
    for hp in range(sk_ref.shape[0]):
        s_ref[0, hp] = _mm(sk_ref[hp], q[:, hp * dq:(hp + 1) * dq], PEER_P_ROUTE, nt=True)


def _peer_query(x, mod, g, qw_hi, qw_lo, sub_keys):
    B, T, D = x.shape
    tm = 256
    nq = qw_hi.shape[1]
    sk = sub_keys.reshape(2 * PEER_H, PEER_NKEYS, -1)
    return pl.pallas_call(
        _peer_query_kernel,
        grid=(B, T // tm),
        in_specs=[
            pl.BlockSpec((1, tm, D), lambda b, t: (b, t, 0)),
            pl.BlockSpec((1, 6, 1, D), lambda b, t: (b, 0, 0, 0)),
            pl.BlockSpec((1, D), lambda b, t: (0, 0)),
            pl.BlockSpec((D, nq), lambda b, t: (0, 0)),
            pl.BlockSpec((D, nq), lambda b, t: (0, 0)),
            pl.BlockSpec(sk.shape, lambda b, t: (0, 0, 0)),
        ],
        out_specs=[pl.BlockSpec((1, D, tm), lambda b, t: (b, 0, t)),
                   pl.BlockSpec((1, 2 * PEER_H, PEER_NKEYS, tm), lambda b, t: (b, 0, 0, t))],
        out_shape=[jax.ShapeDtypeStruct((B, D, T), BF16),
                   jax.ShapeDtypeStruct((B, 2 * PEER_H, PEER_NKEYS, T), F32)],
        compiler_params=_cparams("parallel", "parallel"),
        name="peer_query",
    )(x, mod, g.reshape(1, D), qw_hi, qw_lo, sk)


def _extract_top(s, n, on_pick, stable=True):
    rows = lax.broadcasted_iota(jnp.int32, s.shape, 0).astype(F32)
    big = float(s.shape[0])
    for it in range(n):
        m = jnp.max(s, axis=0, keepdims=True)
        sel = s == m
        if stable:
            idx = jnp.min(jnp.where(sel, rows, big), axis=0, keepdims=True)
            sel = rows == idx
        on_pick(it, m, sel)
        s = jnp.where(sel, -jnp.inf, s)


def _top_ranks(s, rank_scr, val_scr):
    nk = s.shape[0]

    def run(stable):
        rank = [jnp.full(s.shape, float(nk), F32)]

        def pick(it, m, sel):
            val_scr[it:it + 1, :] = m
            rank[0] = jnp.where(sel, float(it), rank[0])

        _extract_top(s, PEER_TOPK, pick, stable=stable)
        rank_scr[...] = rank[0]
        return rank[0]

    rank = run(stable=False)
    n_ranked = jnp.sum(jnp.where(rank < float(PEER_TOPK), 1.0, 0.0), axis=0, keepdims=True)
    over = jnp.broadcast_to(jnp.where(n_ranked > float(PEER_TOPK), 1.0, 0.0), (8, s.shape[1]))

    @pl.when(jnp.max(over) > 0.0)
    def _():
        run(stable=True)


def _dup_bf16(x):
    bits = lax.bitcast_convert_type(x.astype(BF16).astype(F32), jnp.uint32)
    return bits | (bits >> 16)


def _peer_route_kernel(s_ref, rank2_ref, cnt_ref, e1_ref, e2_ref, v1_scr, v2_scr, cand_scr, r1_scr, r2_scr):
    tn = s_ref.shape[3]
    for h in range(PEER_H):
        s1 = s_ref[0, 2 * h]
        s2 = s_ref[0, 2 * h + 1]
        _top_ranks(s1, r1_scr, v1_scr)
        _top_ranks(s2, r2_scr, v2_scr)
        rank1, rank2 = r1_scr[...], r2_scr[...]
        cand_scr[...] = jnp.full(cand_scr.shape, -jnp.inf, F32)
        for ci, (a, b) in enumerate(_CAND):
            cand_scr[ci:ci + 1, :] = v1_scr[a:a + 1, :] + v2_scr[b:b + 1, :]
        picked = [jnp.zeros(cand_scr.shape, F32), jnp.zeros((1, tn), F32), None]

        def pick_c(it, m, sel, picked=picked):
            if it == 0:
                picked[2] = m
            picked[0] = jnp.where(sel, 1.0, picked[0])
            picked[1] = picked[1] + jnp.exp(m - picked[2])

        _extract_top(cand_scr[...], PEER_TOPK, pick_c)
        selmask, z = picked[0], picked[1]
        ca = lax.broadcasted_iota(jnp.int32, (PEER_TOPK, _NCAND_PAD), 0)
        cj = lax.broadcasted_iota(jnp.int32, (PEER_TOPK, _NCAND_PAD), 1)
        row_of = functools.reduce(jnp.logical_or, [(ca == a) & (cj == ci) for ci, (a, _) in enumerate(_CAND)])
        cnt_a = jnp.dot(row_of.astype(BF16), selmask.astype(BF16), preferred_element_type=F32)
        cnt = jnp.zeros(s1.shape, F32)
        for a in range(PEER_TOPK):
            cnt = jnp.where(rank1 == float(a), cnt_a[a:a + 1, :], cnt)
        rank2_ref[0, h] = rank2.astype(BF16)
        cnt_ref[0, h] = _dup_bf16(cnt)
        e1_ref[0, h] = _dup_bf16(jnp.exp(s1 - v1_scr[0:1, :]) / z)
        e2_ref[0, h] = jnp.exp(s2 - v2_scr[0:1, :]).astype(BF16)


def _peer_route(sT):
    B, _, NK, T = sT.shape
    tn = min(512, T)
    ospec = pl.BlockSpec((1, PEER_H, NK, tn), lambda b, t: (b, 0, 0, t))
    oshape = lambda dt: jax.ShapeDtypeStruct((B, PEER_H, NK, T), dt)
    return pl.pallas_call(
        _peer_route_kernel,
        grid=(B, T // tn),
        in_specs=[pl.BlockSpec((1, 2 * PEER_H, NK, tn), lambda b, t: (b, 0, 0, t))],
        out_specs=[ospec] * 4,
        out_shape=[oshape(BF16), oshape(jnp.uint32), oshape(jnp.uint32), oshape(BF16)],
        scratch_shapes=[pltpu.VMEM((PEER_TOPK, tn), F32), pltpu.VMEM((PEER_TOPK, tn), F32),
                        pltpu.VMEM((_NCAND_PAD, tn), F32),
                        pltpu.VMEM((NK, tn), F32), pltpu.VMEM((NK, tn), F32)],
        compiler_params=_cparams("parallel", "parallel"),
        name="peer_route",
    )(sT)


_GELU_C = float(np.sqrt(2.0 / np.pi))
PEER_SUB = 256


def _gelu_tanh(x):
    hx = 0.5 * x
    return hx + hx * jnp.tanh(x * (_GELU_C + (_GELU_C * 0.044715) * (x * x)))


def _row_tile(words):
    return pltpu.bitcast(jnp.broadcast_to(words, (PEER_NKEYS // 2, words.shape[1])), BF16)


def _peer_expert_kernel(ht_ref, rank2_ref, cnt_ref, e1_ref, e2_ref, u_ref, vt_ref, x_ref, mod_ref,
                        fg_ref, o_ref, acc_ref, g_scr, *, te, final_norm):
    e = pl.program_id(2)

    @pl.when(e == 0)
    def _():
        acc_ref[...] = jnp.zeros(acc_ref.shape, F32)

    ht = ht_ref[0]
    rows = te // PEER_NKEYS
    for ii in range(rows):
        i = e * rows + ii
        g = None
        for hd in range(PEER_H):
            cnt_row = _row_tile(cnt_ref[0, hd, pl.ds(i, 1), :])
            e1_row = _row_tile(e1_ref[0, hd, pl.ds(i, 1), :])
            gh = jnp.where(rank2_ref[0, hd] < cnt_row, e2_ref[0, hd] * e1_row, jnp.zeros((), BF16))
            g = gh if g is None else g + gh
        g_scr[ii * PEER_NKEYS:(ii + 1) * PEER_NKEYS, :] = g
    for sb in range(te // PEER_SUB):
        es = slice(sb * PEER_SUB, (sb + 1) * PEER_SUB)
        a = jnp.dot(u_ref[es, :], ht, preferred_element_type=F32)
        g_scr[es, :] = _gelu_tanh(a.astype(BF16)) * g_scr[es, :]
    acc_ref[...] += jnp.dot(vt_ref[...], g_scr[...], preferred_element_type=F32)

    @pl.when(e == pl.num_programs(2) - 1)
    def _():
        y = x_ref[0] + mod_ref[0, 5] * acc_ref[...].T
        if final_norm:
            y = y * lax.rsqrt(jnp.mean(y * y, axis=-1, keepdims=True) + NORM_EPS) * fg_ref[...]
        o_ref[0] = y


def _peer_expert(h2, route, u_bf, vt_bf, x, mod, final_g, final_norm):
    B, T, D = x.shape
    tn = min(512, T)
    te = 2048
    ne = u_bf.shape[0] // te
    rspec = pl.BlockSpec((1, PEER_H, PEER_NKEYS, tn), lambda b, t, e: (b, 0, 0, t))
    return pl.pallas_call(
        functools.partial(_peer_expert_kernel, te=te, final_norm=final_norm),
        grid=(B, T // tn, ne),
        in_specs=[
            pl.BlockSpec((1, D, tn), lambda b, t, e: (b, 0, t)),
            rspec, rspec, rspec, rspec,
            pl.BlockSpec((te, D), lambda b, t, e: (e, 0)),
            pl.BlockSpec((D, te), lambda b, t, e: (0, e)),
            pl.BlockSpec((1, tn, D), lambda b, t, e: (b, t, 0)),
            pl.BlockSpec((1, 6, 1, D), lambda b, t, e: (b, 0, 0, 0)),
            pl.BlockSpec((1, D), lambda b, t, e: (0, 0)),
        ],
        out_specs=pl.BlockSpec((1, tn, D), lambda b, t, e: (b, t, 0)),
        out_shape=jax.ShapeDtypeStruct((B, T, D), F32),
        scratch_shapes=[pltpu.VMEM((D, tn), F32), pltpu.VMEM((te, tn), BF16)],
        compiler_params=_cparams("parallel", "parallel", "arbitrary"),
        name="peer_expert",
    )(h2, *route, u_bf, vt_bf, x, mod, final_g.reshape(1, D))


def _peer(x, mod, g, qw_hi, qw_lo, sub_keys, u_bf, vt_bf, final_g, final_norm):
    h2, sT = _peer_query(x, mod, g, qw_hi, qw_lo, sub_keys)
    route = _peer_route(sT)
    return _peer_expert(h2, route, u_bf, vt_bf, x, mod, final_g, final_norm)


def _split_bf16(w):
    hi = w.astype(BF16)
    return hi, (w - hi.astype(F32)).astype(BF16)


def kernel(x, c, ctx, c_ctx, ada_w, ada_b, norm1_g, norm2_g, w_in, rw_w0, rw_w_up, rw_a0, rw_a_up, rw_g_up,
           rw_k_k, rw_k_a, rw_r_k, rw_lnx_g, na_rpb, sc_conv_w, w_out, peer_q_w, peer_sub_keys, peer_u,
           peer_v, final_g):
    B, S, D = x.shape
    depth = ada_w.shape[0]
    rows = S // GRID_W
    kr = min(NA_KR, rows)
    n_mod = -(-(B + 1) // 8) * 8
    cc = jnp.concatenate([c, c_ctx[None, :], jnp.zeros((n_mod - B - 1, D), F32)], axis=0)
    mod = _ada(cc, ada_w, ada_b)
    xl, xc = x, ctx
    for i in range(depth):
        need_ctx = i < depth - 1
        mod_l = mod[i, :B].reshape(B, 6, 1, D)
        mod_c = jnp.broadcast_to(mod[i, B].reshape(1, 6, 1, D), (B, 6, 1, D))
        w_in_bf = w_in[i].astype(BF16)
        w_out_bf = w_out[i].astype(BF16)
        rw = (rw_w0[i], rw_w_up[i], rw_a0[i], rw_a_up[i], rw_k_k[i], rw_k_a[i])

        zrw_l, q_l, k_l, v_l, zsc_l = _proj_in(xl, mod_l, norm1_g[i], w_in_bf)
        zrw_c, q_c, k_c, v_c, zsc_c = _proj_in(xc, mod_c, norm1_g[i], w_in_bf)
        y2_c, y2_l = _rwkv_state(_rwkv_prep(zrw_c, *rw), _rwkv_prep(zrw_l, *rw))
        yrw_l = _rwkv_out(y2_l, zrw_l, rw_g_up[i], rw_r_k[i], rw_lnx_g[i])
        yna_l = _na_attention(q_l, k_l, v_l, k_c, v_c, _na_bias_table(na_rpb[i], rows, kr))
        ysc_l = _short_conv(zsc_l, sc_conv_w[i])
        xl = _proj_out(yrw_l, yna_l, ysc_l, w_out_bf, xl, mod_l)

        qw_hi, qw_lo = _split_bf16(peer_q_w[i])
        u_bf = peer_u[i].astype(BF16)
        vt_bf = peer_v[i].astype(BF16).T
        peer_args = (norm2_g[i], qw_hi, qw_lo, peer_sub_keys[i], u_bf, vt_bf, final_g)
        xl = _peer(xl, mod_l, *peer_args, final_norm=not need_ctx)
        if need_ctx:
            yrw_c = _rwkv_out(y2_c, zrw_c, rw_g_up[i], rw_r_k[i], rw_lnx_g[i])
            yna_c = _ctx_attention(q_c, k_c, v_c)
            ysc_c = _short_conv(zsc_c, sc_conv_w[i])
            xc = _proj_out(yrw_c, yna_c, ysc_c, w_out_bf, xc, mod_c)
            xc = _peer(xc, mod_c, *peer_args, final_norm=False)
    return xl
```

```python
import functools

import numpy as np
import jax
import jax.numpy as jnp
from jax import lax
from jax.experimental import pallas as pl
from jax.experimental.pallas import tpu as pltpu

F32 = jnp.float32
BF16 = jnp.bfloat16

HEAD_DIM = 64
GRID_W = 64
RW_H = 4
RW_W = RW_H * HEAD_DIM
RW_LW = 64
RW_LA = 64
RW_LG = 128
RW_COLS = 3 * RW_W + 2 * RW_LW + 2 * RW_LA + RW_LG
NA_H = 8
NA_W = NA_H * HEAD_DIM
NA_KR = 8
NA_KC = 16
SC_W = 256
PEER_H = 8
PEER_NKEYS = 128
PEER_TOPK = 16
NORM_EPS = 1e-6
GN_EPS = 64e-5
NEG_INF = -1e30
RW_CHUNK = 64

VMEM_LIMIT = 52 * 1024 * 1024

_CAND = [(a, b) for a in range(PEER_TOPK) for b in range(PEER_TOPK) if (a + 1) * (b + 1) <= PEER_TOPK]
_NCAND = len(_CAND)
_NCAND_PAD = -(-_NCAND // 8) * 8


def _cparams(*sem):
    return pltpu.CompilerParams(dimension_semantics=sem, vmem_limit_bytes=VMEM_LIMIT)


def _mm(a, b, passes=1, nt=False):
    dims = (((1,), (1,)), ((), ())) if nt else (((1,), (0,)), ((), ()))
    dg = functools.partial(lax.dot_general, dimension_numbers=dims, preferred_element_type=F32)
    if passes == 6:
        return dg(a.astype(F32), b.astype(F32), precision=lax.Precision.HIGHEST)
    a_hi = a.astype(BF16)
    b_hi = b.astype(BF16)
    out = dg(a_hi, b_hi)
    if passes == 3:
        a_lo = (a - a_hi.astype(F32)).astype(BF16)
        b_lo = (b - b_hi.astype(F32)).astype(BF16)
        out = out + dg(a_lo, b_hi) + dg(a_hi, b_lo)
    return out


def _sigmoid(x):
    return 1.0 / (1.0 + jnp.exp(-x))


def _norm_mod(x, g, scale, shift):
    ms = jnp.mean(x * x, axis=-1, keepdims=True)
    y = x * lax.rsqrt(ms + NORM_EPS) * g
    return y * (1.0 + scale) + shift


def _ada_kernel(c_ref, w_ref, b_ref, o_ref):
    cc = c_ref[...]
    o_ref[0] = _mm(cc * _sigmoid(cc), w_ref[0], 6) + b_ref[0]


def _ada(cc, ada_w, ada_b):
    L, D, D6 = ada_w.shape
    R = cc.shape[0]
    tn = 1536
    return pl.pallas_call(
        _ada_kernel,
        grid=(L, D6 // tn),
        in_specs=[
            pl.BlockSpec((R, D), lambda l, j: (0, 0)),
            pl.BlockSpec((1, D, tn), lambda l, j: (l, 0, j)),
            pl.BlockSpec((1, 1, tn), lambda l, j: (l, 0, j)),
        ],
        out_specs=pl.BlockSpec((1, R, tn), lambda l, j: (l, 0, j)),
        out_shape=jax.ShapeDtypeStruct((L, R, D6), F32),
        compiler_params=_cparams("parallel", "parallel"),
        name="ada",
    )(cc, ada_w, ada_b.reshape(L, 1, D6))


def _proj_in_kernel(x_ref, mod_ref, g_ref, w_ref, zrw_ref, q_ref, k_ref, v_ref, zsc_ref):
    h = _norm_mod(x_ref[0], g_ref[...], mod_ref[0, 1], mod_ref[0, 0])
    z = jnp.dot(h.astype(BF16), w_ref[...], preferred_element_type=F32)
    o = RW_COLS
    zrw_ref[0] = z[:, :o]
    q_ref[0] = (z[:, o:o + NA_W] * (HEAD_DIM ** -0.5)).astype(BF16)
    k_ref[0] = z[:, o + NA_W:o + 2 * NA_W].astype(BF16)
    v_ref[0] = z[:, o + 2 * NA_W:o + 3 * NA_W].astype(BF16)
    zsc_ref[0] = z[:, o + 3 * NA_W:]


def _proj_in(x, mod, g, w_bf):
    B, T, D = x.shape
    tm = 256
    n_out = w_bf.shape[1]
    row = lambda b, t: (b, t, 0)
    return pl.pallas_call(
        _proj_in_kernel,
        grid=(B, T // tm),
        in_specs=[
            pl.BlockSpec((1, tm, D), row),
            pl.BlockSpec((1, 6, 1, D), lambda b, t: (b, 0, 0, 0)),
            pl.BlockSpec((1, D), lambda b, t: (0, 0)),
            pl.BlockSpec((D, n_out), lambda b, t: (0, 0)),
        ],
        out_specs=[
            pl.BlockSpec((1, tm, RW_COLS), row),
            pl.BlockSpec((1, tm, NA_W), row),
            pl.BlockSpec((1, tm, NA_W), row),
            pl.BlockSpec((1, tm, NA_W), row),
            pl.BlockSpec((1, tm, 3 * SC_W), row),
        ],
        out_shape=[
            jax.ShapeDtypeStruct((B, T, RW_COLS), F32),
            jax.ShapeDtypeStruct((B, T, NA_W), BF16),
            jax.ShapeDtypeStruct((B, T, NA_W), BF16),
            jax.ShapeDtypeStruct((B, T, NA_W), BF16),
            jax.ShapeDtypeStruct((B, T, 3 * SC_W), F32),
        ],
        compiler_params=_cparams("parallel", "parallel"),
        name="proj_in",
    )(x, mod, g.reshape(1, D), w_bf)


RW_CHUNKS_PER_STEP = 2
RW_P_ALG = 1
RW_P_LORA = 3


def _rwkv_prep_kernel(z_ref, w0_ref, wup_ref, a0_ref, aup_ref, kk_ref, ka_ref,
                      phi_ref, zc_ref, wr_ref, yin_ref):
    C = RW_CHUNK
    W = RW_W
    assert C == HEAD_DIM
    N = RW_H * C
    row = lax.broadcasted_iota(jnp.int32, (C, C), 0)
    col = lax.broadcasted_iota(jnp.int32, (C, C), 1)
    brow = lax.broadcasted_iota(jnp.int32, (N, W), 0)
    bcol = lax.broadcasted_iota(jnp.int32, (N, W), 1)
    bd = (brow // C) == (bcol // HEAD_DIM)
    eye_j = lax.broadcasted_iota(jnp.int32, (HEAD_DIM, W), 0) == \
        lax.broadcasted_iota(jnp.int32, (HEAD_DIM, W), 1) % HEAD_DIM
    head_ones = bd.astype(F32)
    mm = functools.partial(_mm, passes=RW_P_ALG)
    mm_tn = lambda a, b: lax.dot_general(a.astype(BF16), b.astype(BF16), (((0,), (0,)), ((), ())),
                                         preferred_element_type=F32)
    collapse = lambda m: functools.reduce(lambda a, b: a + b, [m[h * C:(h + 1) * C] for h in range(RW_H)])

    def direction(ci, d):
        z = z_ref[0, ci * C:(ci + 1) * C, :]
        r = z[:, 0:W]
        k = z[:, W:2 * W]
        v = z[:, 2 * W:3 * W]
        zw = z[:, 3 * W:3 * W + 2 * RW_LW]
        za = z[:, 3 * W + 2 * RW_LW:3 * W + 2 * RW_LW + 2 * RW_LA]
        kk = k * kk_ref[...]
        kk = kk * lax.rsqrt(_mm(kk * kk, head_ones, 3) + 1e-12)
        lw = jnp.tanh(zw[:, d * RW_LW:(d + 1) * RW_LW])
        la = za[:, d * RW_LA:(d + 1) * RW_LA]
        w_raw = w0_ref[d:d + 1, :] + _mm(lw, wup_ref[d], RW_P_LORA)
        softplus = jnp.maximum(-w_raw, 0.0) + jnp.log(1.0 + jnp.exp(-jnp.abs(w_raw)))
        logw = -jnp.exp(-softplus - 0.5)
        a = _sigmoid(a0_ref[d:d + 1, :] + _mm(la, aup_ref[d], RW_P_LORA))
        k_d = k * (1.0 + (a - 1.0) * ka_ref[...])
        b_d = kk * a
        if d == 0:
            tri_incl, tri_strict = col <= row, col < row
        else:
            tri_incl, tri_strict = col >= row, col > row
        cum = _mm(tri_incl.astype(F32), logw, 6)
        tot = jnp.sum(logw, axis=0, keepdims=True)
        e_neg = jnp.exp(-cum)
        e_end = jnp.exp(tot - cum)
        a_t = -kk * jnp.exp(cum - logw)
        r_t = r * jnp.exp(cum)
        b_t = b_d * e_neg
        k_t = k_d * e_neg
        b_p = b_d * e_end
        k_p = k_d * e_end
        p_c = jnp.exp(tot)
        bdx = lambda x: jnp.where(bd, jnp.concatenate([x] * RW_H, axis=0), 0.0)
        a_bd, r_bd, v_bd = bdx(a_t), bdx(r_t), bdx(v)
        yield
        p4 = mm(jnp.concatenate([a_bd, r_bd], axis=0),
                jnp.concatenate([bdx(b_t), bdx(k_t)], axis=0), nt=True)
        yield
        tri_strict_bd = bd & ((bcol < brow) if d == 0 else (bcol > brow))
        tri_incl_bd = bd & ((bcol <= brow) if d == 0 else (bcol >= brow))
        l_ab = jnp.where(tri_strict_bd, p4[:N, :N], 0.0)
        l_ak = jnp.where(tri_strict_bd, p4[:N, N:], 0.0)
        m_rb = jnp.where(tri_incl_bd, p4[N:, :N], 0.0)
        m_rk = jnp.where(tri_incl_bd, p4[N:, N:], 0.0)
        t_inv = jnp.where(brow == bcol, 1.0, l_ab)
        pw = l_ab
        n = 2
        lakv = mm(l_ak, v_bd)
        mrkv = mm(m_rk, v_bd)
        kpv = mm_tn(bdx(k_p), v_bd)
        while n < C:
            pw = mm(pw, pw)
            yield
            t_inv = t_inv + mm(t_inv, pw)
            n *= 2
        yield
        x = mm(t_inv, jnp.concatenate([a_bd, lakv], axis=1))
        yield
        wy = mm(m_rb, x)
        pz = mm_tn(bdx(b_p), x)
        wr_ref[0, ci, d] = r_t + collapse(wy[:, :W])
        yin_ref[0, ci, d] = collapse(wy[:, W:] + mrkv)
        phi_ref[0, ci, d] = jnp.where(eye_j, p_c, 0.0) + collapse(pz[:, :W])
        zc_ref[0, ci, d] = collapse(pz[:, W:] + kpv)

    live = [direction(ci, d) for ci in range(z_ref.shape[1] // C) for d in range(2)]
    done = object()
    while live:
        live = [g for g in live if next(g, done) is not done]


def _rwkv_prep(zrw, w0, w_up, a0, a_up, k_k, k_a):
    B, T, _ = zrw.shape
    C = RW_CHUNK
    nch = T // C
    full = lambda a: pl.BlockSpec(a.shape, lambda b, c: (0,) * a.ndim)
    k_k = k_k.reshape(1, RW_W)
    k_a = k_a.reshape(1, RW_W)
    cps = RW_CHUNKS_PER_STEP
    ospec = pl.BlockSpec((1, cps, 2, HEAD_DIM, RW_W), lambda b, c: (b, c, 0, 0, 0))
    oshape = jax.ShapeDtypeStruct((B, nch, 2, HEAD_DIM, RW_W), F32)
    return pl.pallas_call(
        _rwkv_prep_kernel,
        grid=(B, nch // cps),
        in_specs=[pl.BlockSpec((1, cps * C, RW_COLS), lambda b, c: (b, c, 0)),
                  full(w0), full(w_up), full(a0), full(a_up), full(k_k), full(k_a)],
        out_specs=[ospec] * 4,
        out_shape=[oshape] * 4,
        compiler_params=_cparams("parallel", "parallel"),
        name="rwkv_prep",
    )(zrw, w0, w_up, a0, a_up, k_k, k_a)


RW_P_STATE = 3


def _rwkv_state_kernel(phic, zcc, wrc, yinc, phil, zcl, wrl, yinl, yc_ref, yl_ref):
    mm = functools.partial(_mm, passes=RW_P_STATE)

    N = RW_H * HEAD_DIM
    bd = (lax.broadcasted_iota(jnp.int32, (N, RW_W), 0) // HEAD_DIM) == \
        (lax.broadcasted_iota(jnp.int32, (N, RW_W), 1) // HEAD_DIM)

    def run(Hs, phi, zc, wr, yin, y_ref):
        nch = phi.shape[1]

        def body(s, Hs):
            new = []
            for d, H in enumerate(Hs):
                c = s if d == 0 else nch - 1 - s
                t0 = pl.multiple_of(c * RW_CHUNK, RW_CHUNK)
                h_bd = jnp.where(bd, jnp.concatenate([H] * RW_H, axis=0), 0.0)
                ops = jnp.concatenate([wr[0, c, d], phi[0, c, d]], axis=0)
                out = mm(ops, h_bd) + jnp.concatenate([yin[0, c, d], zc[0, c, d]], axis=0)
                y_ref[0, d, pl.ds(t0, RW_CHUNK), :] = out[:RW_CHUNK]
                new.append(out[RW_CHUNK:])
            return tuple(new)

        return lax.fori_loop(0, nch, body, Hs)

    Hs = (jnp.zeros((HEAD_DIM, RW_W), F32),) * 2
    Hs = run(Hs, phic, zcc, wrc, yinc, yc_ref)
    run(Hs, phil, zcl, wrl, yinl, yl_ref)


def _rwkv_state(ops_c, ops_l):
    B, nc = ops_c[0].shape[:2]
    nl = ops_l[0].shape[1]
    spec = lambda n: pl.BlockSpec((1, n, 2, HEAD_DIM, RW_W), lambda b: (b, 0, 0, 0, 0))
    yspec = lambda n: pl.BlockSpec((1, 2, n * RW_CHUNK, RW_W), lambda b: (b, 0, 0, 0))
    return pl.pallas_call(
        _rwkv_state_kernel,
        grid=(B,),
        in_specs=[spec(nc)] * 4 + [spec(nl)] * 4,
        out_specs=[yspec(nc), yspec(nl)],
        out_shape=[jax.ShapeDtypeStruct((B, 2, nc * RW_CHUNK, RW_W), F32),
                   jax.ShapeDtypeStruct((B, 2, nl * RW_CHUNK, RW_W), F32)],
        compiler_params=_cparams("parallel"),
        name="rwkv_state",
    )(*ops_c, *ops_l)


def _rwkv_out_kernel(yf_ref, yr_ref, z_ref, gup_ref, rk_ref, lnx_ref, o_ref):
    W = RW_W
    y = yf_ref[0, 0] + yr_ref[0, 0]
    z = z_ref[0]
    r, k, v = z[:, 0:W], z[:, W:2 * W], z[:, 2 * W:3 * W]
    zg = z[:, RW_COLS - RW_LG:RW_COLS]
    hrow = lax.broadcasted_iota(jnp.int32, (W, W), 0) // HEAD_DIM
    hcol = lax.broadcasted_iota(jnp.int32, (W, W), 1) // HEAD_DIM
    head_ones = (hrow == hcol).astype(F32)
    inv = 1.0 / HEAD_DIM
    mu = _mm(y, head_ones, 3) * inv
    yc = y - mu
    var = _mm(yc * yc, head_ones, 3) * inv
    yn = yc * lax.rsqrt(var + GN_EPS) * lnx_ref[...]
    bonus = _mm(r * k * rk_ref[...], head_ones, 3)
    yn = yn + bonus * v
    gate = _mm(_sigmoid(zg), gup_ref[...], 3)
    o_ref[0] = yn * gate


def _rwkv_out(y2, zrw, g_up, r_k, lnx_g):
    B, _, T, W = y2.shape
    tm = 256
    return pl.pallas_call(
        _rwkv_out_kernel,
        grid=(B, T // tm),
        in_specs=[
            pl.BlockSpec((1, 1, tm, W), lambda b, t: (b, 0, t, 0)),
            pl.BlockSpec((1, 1, tm, W), lambda b, t: (b, 1, t, 0)),
            pl.BlockSpec((1, tm, RW_COLS), lambda b, t: (b, t, 0)),
            pl.BlockSpec((RW_LG, W), lambda b, t: (0, 0)),
            pl.BlockSpec((1, W), lambda b, t: (0, 0)),
            pl.BlockSpec((1, W), lambda b, t: (0, 0)),
        ],
        out_specs=pl.BlockSpec((1, tm, W), lambda b, t: (b, t, 0)),
        out_shape=jax.ShapeDtypeStruct((B, T, W), F32),
        compiler_params=_cparams("parallel", "parallel"),
        name="rwkv_out",
    )(y2, y2, zrw, g_up, r_k.reshape(1, W), lnx_g.reshape(1, W))


def _softmax_av_steps(scores, values, store):
    m = functools.reduce(jnp.maximum, [jnp.max(s, axis=-1, keepdims=True) for s in scores])
    yield
    ps = [jnp.exp(s - m) for s in scores]
    l = functools.reduce(lambda a, b: a + b, [jnp.sum(p, axis=-1, keepdims=True) for p in ps])
    yield
    o = functools.reduce(lambda a, b: a + b,
                         [jnp.dot(p.astype(BF16), v, preferred_element_type=F32) for p, v in zip(ps, values)])
    yield
    store(o / l)


def _interleave(chains):
    live = list(chains)
    done = object()
    while live:
        live = [g for g in live if next(g, done) is not done]


NA_HEADS_INTERLEAVED = 8


def _na_kernel(q_ref, k_ref, v_ref, kc_ref, vc_ref, bias_ref, o_ref, *, kr):
    r = pl.program_id(1)
    rows = pl.num_programs(1)
    start = pl.multiple_of(jnp.clip(r - NA_KR // 2, 0, rows - kr) * GRID_W, GRID_W)
    nwin = kr * GRID_W

    def head(h):
        sl = slice(h * HEAD_DIM, (h + 1) * HEAD_DIM)
        q = q_ref[0, :, sl]
        kb = k_ref[0, pl.ds(start, nwin), sl]
        vb = v_ref[0, pl.ds(start, nwin), sl]
        s_win = _mm(q, kb, nt=True) + bias_ref[0, h]
        s_ctx = _mm(q, kc_ref[0, :, sl], nt=True)
        yield

        def store(o):
            o_ref[0, :, sl] = o

        yield from _softmax_av_steps([s_win, s_ctx], [vb, vc_ref[0, :, sl]], store)

    for h0 in range(0, NA_H, NA_HEADS_INTERLEAVED):
        _interleave(head(h) for h in range(h0, h0 + NA_HEADS_INTERLEAVED))


def _na_bias_table(rpb, rows, kr):
    ci = np.arange(GRID_W)
    c0 = np.clip(ci - NA_KC // 2, 0, GRID_W - NA_KC)
    col_ok = (ci[None, :] >= c0[:, None]) & (ci[None, :] < c0[:, None] + NA_KC)
    dc = np.clip(ci[None, :] - ci[:, None] + (NA_KC - 1), 0, 2 * NA_KC - 2)
    n_off = NA_KR
    dr = np.arange(kr)[None, :] - np.arange(n_off)[:, None] + (NA_KR - 1)
    dr_ok = (dr >= 0) & (dr < 2 * NA_KR - 1)
    sel_r = (dr[:, :, None] == np.arange(2 * NA_KR - 1)).astype(np.float32)
    sel_c = (dc[:, :, None] == np.arange(2 * NA_KC - 1)).astype(np.float32)
    bias = jnp.einsum('hrc,okr,qwc->ohqkw', rpb.astype(F32), sel_r, sel_c, precision=lax.Precision.HIGHEST)
    ok = dr_ok[:, None, None, :, None] & col_ok[None, None, :, None, :]
    bias = jnp.where(ok, bias, NEG_INF)
    return bias.reshape(n_off, NA_H, GRID_W, kr * GRID_W)


def _na_attention(q, k, v, kc, vc, bias_tab):
    B, S, W = q.shape
    Cn = kc.shape[1]
    rows = S // GRID_W
    kr = min(NA_KR, rows)
    nwin = kr * GRID_W

    def bias_idx(b, r):
        return (r - jnp.clip(r - NA_KR // 2, 0, rows - kr), 0, 0, 0)

    return pl.pallas_call(
        functools.partial(_na_kernel, kr=kr),
        grid=(B, rows),
        in_specs=[
            pl.BlockSpec((1, GRID_W, W), lambda b, r: (b, r, 0)),
            pl.BlockSpec((1, S, W), lambda b, r: (b, 0, 0)),
            pl.BlockSpec((1, S, W), lambda b, r: (b, 0, 0)),
            pl.BlockSpec((1, Cn, W), lambda b, r: (b, 0, 0)),
            pl.BlockSpec((1, Cn, W), lambda b, r: (b, 0, 0)),
            pl.BlockSpec((1, NA_H, GRID_W, nwin), bias_idx),
        ],
        out_specs=pl.BlockSpec((1, GRID_W, W), lambda b, r: (b, r, 0)),
        out_shape=jax.ShapeDtypeStruct((B, S, W), F32),
        compiler_params=_cparams("parallel", "arbitrary"),
        name="na_attn",
    )(q, k, v, kc, vc, bias_tab)


def _ctx_attn_kernel(q_ref, k_ref, v_ref, o_ref):
    def head(h):
        sl = slice(h * HEAD_DIM, (h + 1) * HEAD_DIM)
        s = _mm(q_ref[0, :, sl], k_ref[0, :, sl], nt=True)
        yield

        def store(o):
            o_ref[0, :, sl] = o

        yield from _softmax_av_steps([s], [v_ref[0, :, sl]], store)

    for h0 in range(0, NA_H, NA_HEADS_INTERLEAVED):
        _interleave(head(h) for h in range(h0, h0 + NA_HEADS_INTERLEAVED))


def _ctx_attention(q, k, v):
    B, Cn, W = q.shape
    spec = pl.BlockSpec((1, Cn, W), lambda b: (b, 0, 0))
    return pl.pallas_call(
        _ctx_attn_kernel,
        grid=(B,),
        in_specs=[spec] * 3,
        out_specs=spec,
        out_shape=jax.ShapeDtypeStruct((B, Cn, W), F32),
        compiler_params=_cparams("parallel"),
        name="ctx_attn",
    )(q, k, v)


def _conv_kernel(z_ref, w_ref, o_ref):
    T = z_ref.shape[1]
    z = z_ref[0]
    zb, zc, zx = z[:, :SC_W], z[:, SC_W:2 * SC_W], z[:, 2 * SC_W:]
    u = zc * zx
    t = lax.broadcasted_iota(jnp.int32, u.shape, 0)
    prev = jnp.where(t == 0, 0.0, pltpu.roll(u, 1, 0))
    nxt = jnp.where(t == T - 1, 0.0, pltpu.roll(u, T - 1, 0))
    o_ref[0] = zb * (prev * w_ref[0:1, :] + u * w_ref[1:2, :] + nxt * w_ref[2:3, :])


def _short_conv(zsc, w):
    B, T, _ = zsc.shape
    return pl.pallas_call(
        _conv_kernel,
        grid=(B,),
        in_specs=[pl.BlockSpec((1, T, 3 * SC_W), lambda b: (b, 0, 0)),
                  pl.BlockSpec((3, SC_W), lambda b: (0, 0))],
        out_specs=pl.BlockSpec((1, T, SC_W), lambda b: (b, 0, 0)),
        out_shape=jax.ShapeDtypeStruct((B, T, SC_W), F32),
        compiler_params=_cparams("parallel"),
        name="short_conv",
    )(zsc, w)


def _proj_out_kernel(yrw_ref, yna_ref, ysc_ref, w_ref, x_ref, mod_ref, o_ref):
    a, b = RW_W, RW_W + NA_W
    y = (jnp.dot(yrw_ref[0].astype(BF16), w_ref[:a, :], preferred_element_type=F32)
         + jnp.dot(yna_ref[0].astype(BF16), w_ref[a:b, :], preferred_element_type=F32)
         + jnp.dot(ysc_ref[0].astype(BF16), w_ref[b:, :], preferred_element_type=F32))
    o_ref[0] = x_ref[0] + mod_ref[0, 2] * y


def _proj_out(yrw, yna, ysc, w_bf, x, mod):
    B, T, D = x.shape
    tm = 256
    row = lambda b, t: (b, t, 0)
    return pl.pallas_call(
        _proj_out_kernel,
        grid=(B, T // tm),
        in_specs=[
            pl.BlockSpec((1, tm, RW_W), row),
            pl.BlockSpec((1, tm, NA_W), row),
            pl.BlockSpec((1, tm, SC_W), row),
            pl.BlockSpec(w_bf.shape, lambda b, t: (0, 0)),
            pl.BlockSpec((1, tm, D), row),
            pl.BlockSpec((1, 6, 1, D), lambda b, t: (b, 0, 0, 0)),
        ],
        out_specs=pl.BlockSpec((1, tm, D), row),
        out_shape=jax.ShapeDtypeStruct((B, T, D), F32),
        compiler_params=_cparams("parallel", "parallel"),
        name="proj_out",
    )(yrw, yna, ysc, w_bf, x, mod)


PEER_P_ROUTE = 3


def _peer_query_kernel(x_ref, mod_ref, g_ref, wh_ref, wl_ref, sk_ref, ht_ref, s_ref):
    h = _norm_mod(x_ref[0], g_ref[...], mod_ref[0, 4], mod_ref[0, 3])
    h_hi = h.astype(BF16)
    ht_ref[0] = h.T.astype(BF16)
    q = jnp.dot(h_hi, wh_ref[...], preferred_element_type=F32)
    if PEER_P_ROUTE == 3:
        h_lo = (h - h_hi.astype(F32)).astype(BF16)
        q = q + jnp.dot(h_lo, wh_ref[...], preferred_element_type=F32) \
            + jnp.dot(h_hi, wl_ref[...], preferred_element_type=F32)
    dq = sk_ref.shape[2]
    for hp in range(sk_ref.shape[0]):
        s_ref[0, hp] = _mm(sk_ref[hp], q[:, hp * dq:(hp + 1) * dq], PEER_P_ROUTE, nt=True)


def _peer_query(x, mod, g, qw_hi, qw_lo, sub_keys):
    B, T, D = x.shape
    tm = 256
    nq = qw_hi.shape[1]
    sk = sub_keys.reshape(2 * PEER_H, PEER_NKEYS, -1)
    return pl.pallas_call(
        _peer_query_kernel,
        grid=(B, T // tm),
        in_specs=[
            pl.BlockSpec((1, tm, D), lambda b, t: (b, t, 0)),
            pl.BlockSpec((1, 6, 1, D), lambda b, t: (b, 0, 0, 0)),
            pl.BlockSpec((1, D), lambda b, t: (0, 0)),
            pl.BlockSpec((D, nq), lambda b, t: (0, 0)),
            pl.BlockSpec((D, nq), lambda b, t: (0, 0)),
            pl.BlockSpec(sk.shape, lambda b, t: (0, 0, 0)),
        ],
        out_specs=[pl.BlockSpec((1, D, tm), lambda b, t: (b, 0, t)),
                   pl.BlockSpec((1, 2 * PEER_H, PEER_NKEYS, tm), lambda b, t: (b, 0, 0, t))],
        out_shape=[jax.ShapeDtypeStruct((B, D, T), BF16),
                   jax.ShapeDtypeStruct((B, 2 * PEER_H, PEER_NKEYS, T), F32)],
        compiler_params=_cparams("parallel", "parallel"),
        name="peer_query",
    )(x, mod, g.reshape(1, D), qw_hi, qw_lo, sk)


def _extract_top(s, n, on_pick, stable=True):
    rows = lax.broadcasted_iota(jnp.int32, s.shape, 0).astype(F32)
    big = float(s.shape[0])
    for it in range(n):
        m = jnp.max(s, axis=0, keepdims=True)
        sel = s == m
        if stable:
            idx = jnp.min(jnp.where(sel, rows, big), axis=0, keepdims=True)
            sel = rows == idx
        on_pick(it, m, sel)
        s = jnp.where(sel, -jnp.inf, s)


def _batcher_pairs(lo, hi):
    def merge(lo, hi, r):
        step = r * 2
        if step < hi - lo:
            yield from merge(lo, hi, step)
            yield from merge(lo + r, hi, step)
            yield from [(i, i + r) for i in range(lo + r, hi - r, step)]
        else:
            yield (lo, lo + r)

    if hi - lo >= 1:
        mid = lo + (hi - lo) // 2
        yield from _batcher_pairs(lo, mid)
        yield from _batcher_pairs(mid + 1, hi)
        yield from merge(lo, hi, 1)


def _sorted_top(s):
    assert s.shape[0] == 8 * PEER_TOPK

    def cmpx(v, i, j):
        v[i], v[j] = jnp.maximum(v[i], v[j]), jnp.minimum(v[i], v[j])

    v = [s[8 * g:8 * g + 8] for g in range(PEER_TOPK)]
    for i, j in _batcher_pairs(0, PEER_TOPK - 1):
        cmpx(v, i, j)
    for shift in (4, 2, 1):
        v = [jnp.maximum(v[i], pltpu.roll(v[PEER_TOPK - 1 - i], shift, 0)) for i in range(PEER_TOPK)]
        stride = PEER_TOPK // 2
        while stride:
            for i in range(PEER_TOPK):
                if not i & stride:
                    cmpx(v, i, i + stride)
            stride //= 2
    return v


def _top_ranks(s, rank_scr, val_scr):
    nk = s.shape[0]
    v = _sorted_top(s)
    groups = [s[8 * g:8 * g + 8] for g in range(nk // 8)]
    ranks = []
    n_ge = None
    for grp in groups:
        r = jnp.full(grp.shape, float(nk), F32)
        for a in reversed(range(PEER_TOPK)):
            r = jnp.where(grp >= v[a], float(a), r)
        ranks.append(r)
        ge = jnp.where(grp >= v[PEER_TOPK - 1], 1.0, 0.0)
        n_ge = ge if n_ge is None else n_ge + ge
    rank_scr[...] = jnp.concatenate(ranks, axis=0)
    for a in range(PEER_TOPK):
        val_scr[a:a + 1, :] = v[a][0:1, :]
    gap = functools.reduce(jnp.minimum, [v[a] - v[a + 1] for a in range(PEER_TOPK - 1)])
    tied = jnp.where(gap[0:1, :] > 0.0, 0.0, 1.0) \
        + jnp.where(jnp.sum(n_ge, axis=0, keepdims=True) > float(PEER_TOPK), 1.0, 0.0)

    @pl.when(jnp.max(jnp.broadcast_to(tied, (8, s.shape[1]))) > 0.0)
    def _():
        rank = [jnp.full(s.shape, float(nk), F32)]

        def pick(it, m, sel):
            val_scr[it:it + 1, :] = m
            rank[0] = jnp.where(sel, float(it), rank[0])

        _extract_top(s, PEER_TOPK, pick)
        rank_scr[...] = rank[0]


def _dup_bf16(x):
    bits = lax.bitcast_convert_type(x.astype(BF16).astype(F32), jnp.uint32)
    return bits | (bits >> 16)


def _peer_route_kernel(s_ref, rank2_ref, cnt_ref, e1_ref, e2_ref, v1_scr, v2_scr, cand_scr, r1_scr, r2_scr):
    tn = s_ref.shape[3]
    for h in range(PEER_H):
        s1 = s_ref[0, 2 * h]
        s2 = s_ref[0, 2 * h + 1]
        _top_ranks(s1, r1_scr, v1_scr)
        _top_ranks(s2, r2_scr, v2_scr)
        rank1, rank2 = r1_scr[...], r2_scr[...]
        cand_scr[...] = jnp.full(cand_scr.shape, -jnp.inf, F32)
        for ci, (a, b) in enumerate(_CAND):
            cand_scr[ci:ci + 1, :] = v1_scr[a:a + 1, :] + v2_scr[b:b + 1, :]
        picked = [jnp.zeros(cand_scr.shape, F32), jnp.zeros((1, tn), F32), None]

        def pick_c(it, m, sel, picked=picked):
            if it == 0:
                picked[2] = m
            picked[0] = jnp.where(sel, 1.0, picked[0])
            picked[1] = picked[1] + jnp.exp(m - picked[2])

        _extract_top(cand_scr[...], PEER_TOPK, pick_c)
        selmask, z = picked[0], picked[1]
        ca = lax.broadcasted_iota(jnp.int32, (PEER_TOPK, _NCAND_PAD), 0)
        cj = lax.broadcasted_iota(jnp.int32, (PEER_TOPK, _NCAND_PAD), 1)
        row_of = functools.reduce(jnp.logical_or, [(ca == a) & (cj == ci) for ci, (a, _) in enumerate(_CAND)])
        cnt_a = jnp.dot(row_of.astype(BF16), selmask.astype(BF16), preferred_element_type=F32)
        cnt = jnp.zeros(s1.shape, F32)
        for a in range(PEER_TOPK):
            cnt = jnp.where(rank1 == float(a), cnt_a[a:a + 1, :], cnt)
        rank2_ref[0, h] = rank2.astype(BF16)
        cnt_ref[0, h] = _dup_bf16(cnt)
        e1_ref[0, h] = _dup_bf16(jnp.exp(s1 - v1_scr[0:1, :]) / z)
        e2_ref[0, h] = jnp.exp(s2 - v2_scr[0:1, :]).astype(BF16)


def _peer_route(sT):
    B, _, NK, T = sT.shape
    tn = 256
    ospec = pl.BlockSpec((1, PEER_H, NK, tn), lambda b, t: (b, 0, 0, t))
    oshape = lambda dt: jax.ShapeDtypeStruct((B, PEER_H, NK, T), dt)
    return pl.pallas_call(
        _peer_route_kernel,
        grid=(B, T // tn),
        in_specs=[pl.BlockSpec((1, 2 * PEER_H, NK, tn), lambda b, t: (b, 0, 0, t))],
        out_specs=[ospec] * 4,
        out_shape=[oshape(BF16), oshape(jnp.uint32), oshape(jnp.uint32), oshape(BF16)],
        scratch_shapes=[pltpu.VMEM((PEER_TOPK, tn), F32), pltpu.VMEM((PEER_TOPK, tn), F32),
                        pltpu.VMEM((_NCAND_PAD, tn), F32),
                        pltpu.VMEM((NK, tn), F32), pltpu.VMEM((NK, tn), F32)],
        compiler_params=_cparams("parallel", "parallel"),
        name="peer_route",
    )(sT)


_GELU_C = float(np.sqrt(2.0 / np.pi))
PEER_SUB = 256


def _gelu_tanh(x):
    hx = 0.5 * x
    return hx + hx * jnp.tanh(x * (_GELU_C + (_GELU_C * 0.044715) * (x * x)))


def _row_tile(words):
    return pltpu.bitcast(jnp.broadcast_to(words, (PEER_NKEYS // 2, words.shape[1])), BF16)


def _peer_expert_kernel(ht_ref, rank2_ref, cnt_ref, e1_ref, e2_ref, u_ref, vt_ref, x_ref, mod_ref,
                        fg_ref, o_ref, acc_ref, g_scr, *, te, final_norm):
    e = pl.program_id(2)

    @pl.when(e == 0)
    def _():
        acc_ref[...] = jnp.zeros(acc_ref.shape, F32)

    ht = ht_ref[0]
    rows = te // PEER_NKEYS
    for ii in range(rows):
        i = e * rows + ii
        g = None
        for hd in range(PEER_H):
            cnt_row = _row_tile(cnt_ref[0, hd, pl.ds(i, 1), :])
            e1_row = _row_tile(e1_ref[0, hd, pl.ds(i, 1), :])
            gh = jnp.where(rank2_ref[0, hd] < cnt_row, e2_ref[0, hd] * e1_row, jnp.zeros((), BF16))
            g = gh if g is None else g + gh
        g_scr[ii * PEER_NKEYS:(ii + 1) * PEER_NKEYS, :] = g
    for sb in range(te // PEER_SUB):
        es = slice(sb * PEER_SUB, (sb + 1) * PEER_SUB)
        a = jnp.dot(u_ref[es, :], ht, preferred_element_type=F32)
        g_scr[es, :] = _gelu_tanh(a.astype(BF16)) * g_scr[es, :]
    acc_ref[...] += jnp.dot(vt_ref[...], g_scr[...], preferred_element_type=F32)

    @pl.when(e == pl.num_programs(2) - 1)
    def _():
        y = x_ref[0] + mod_ref[0, 5] * acc_ref[...].T
        if final_norm:
            y = y * lax.rsqrt(jnp.mean(y * y, axis=-1, keepdims=True) + NORM_EPS) * fg_ref[...]
        o_ref[0] = y


def _peer_expert(h2, route, u_bf, vt_bf, x, mod, final_g, final_norm):
    B, T, D = x.shape
    tn = min(512, T)
    te = 2048
    ne = u_bf.shape[0] // te
    rspec = pl.BlockSpec((1, PEER_H, PEER_NKEYS, tn), lambda b, t, e: (b, 0, 0, t))
    return pl.pallas_call(
        functools.partial(_peer_expert_kernel, te=te, final_norm=final_norm),
        grid=(B, T // tn, ne),
        in_specs=[
            pl.BlockSpec((1, D, tn), lambda b, t, e: (b, 0, t)),
            rspec, rspec, rspec, rspec,
            pl.BlockSpec((te, D), lambda b, t, e: (e, 0)),
            pl.BlockSpec((D, te), lambda b, t, e: (0, e)),
            pl.BlockSpec((1, tn, D), lambda b, t, e: (b, t, 0)),
            pl.BlockSpec((1, 6, 1, D), lambda b, t, e: (b, 0, 0, 0)),
            pl.BlockSpec((1, D), lambda b, t, e: (0, 0)),
        ],
        out_specs=pl.BlockSpec((1, tn, D), lambda b, t, e: (b, t, 0)),
        out_shape=jax.ShapeDtypeStruct((B, T, D), F32),
        scratch_shapes=[pltpu.VMEM((D, tn), F32), pltpu.VMEM((te, tn), BF16)],
        compiler_params=_cparams("parallel", "parallel", "arbitrary"),
        name="peer_expert",
    )(h2, *route, u_bf, vt_bf, x, mod, final_g.reshape(1, D))


def _peer(x, mod, g, qw_hi, qw_lo, sub_keys, u_bf, vt_bf, final_g, final_norm):
    h2, sT = _peer_query(x, mod, g, qw_hi, qw_lo, sub_keys)
    route = _peer_route(sT)
    return _peer_expert(h2, route, u_bf, vt_bf, x, mod, final_g, final_norm)


def _split_bf16(w):
    hi = w.astype(BF16)
    return hi, (w - hi.astype(F32)).astype(BF16)


def kernel(x, c, ctx, c_ctx, ada_w, ada_b, norm1_g, norm2_g, w_in, rw_w0, rw_w_up, rw_a0, rw_a_up, rw_g_up,
           rw_k_k, rw_k_a, rw_r_k, rw_lnx_g, na_rpb, sc_conv_w, w_out, peer_q_w, peer_sub_keys, peer_u,
           peer_v, final_g):
    B, S, D = x.shape
    depth = ada_w.shape[0]
    rows = S // GRID_W
    kr = min(NA_KR, rows)
    n_mod = -(-(B + 1) // 8) * 8
    cc = jnp.concatenate([c, c_ctx[None, :], jnp.zeros((n_mod - B - 1, D), F32)], axis=0)
    mod = _ada(cc, ada_w, ada_b)
    xl, xc = x, ctx
    for i in range(depth):
        need_ctx = i < depth - 1
        mod_l = mod[i, :B].reshape(B, 6, 1, D)
        mod_c = jnp.broadcast_to(mod[i, B].reshape(1, 6, 1, D), (B, 6, 1, D))
        w_in_bf = w_in[i].astype(BF16)
        w_out_bf = w_out[i].astype(BF16)
        rw = (rw_w0[i], rw_w_up[i], rw_a0[i], rw_a_up[i], rw_k_k[i], rw_k_a[i])

        zrw_l, q_l, k_l, v_l, zsc_l = _proj_in(xl, mod_l, norm1_g[i], w_in_bf)
        zrw_c, q_c, k_c, v_c, zsc_c = _proj_in(xc, mod_c, norm1_g[i], w_in_bf)
        y2_c, y2_l = _rwkv_state(_rwkv_prep(zrw_c, *rw), _rwkv_prep(zrw_l, *rw))
        yrw_l = _rwkv_out(y2_l, zrw_l, rw_g_up[i], rw_r_k[i], rw_lnx_g[i])
        yna_l = _na_attention(q_l, k_l, v_l, k_c, v_c, _na_bias_table(na_rpb[i], rows, kr))
        ysc_l = _short_conv(zsc_l, sc_conv_w[i])
        xl = _proj_out(yrw_l, yna_l, ysc_l, w_out_bf, xl, mod_l)

        qw_hi, qw_lo = _split_bf16(peer_q_w[i])
        u_bf = peer_u[i].astype(BF16)
        vt_bf = peer_v[i].astype(BF16).T
        peer_args = (norm2_g[i], qw_hi, qw_lo, peer_sub_keys[i], u_bf, vt_bf, final_g)
        xl = _peer(xl, mod_l, *peer_args, final_norm=not need_ctx)
        if need_ctx:
            yrw_c = _rwkv_out(y2_c, zrw_c, rw_g_up[i], rw_r_k[i], rw_lnx_g[i])
            yna_c = _ctx_attention(q_c, k_c, v_c)
            ysc_c = _short_conv(zsc_c, sc_conv_w[i])
            xc = _proj_out(yrw_c, yna_c, ysc_c, w_out_bf, xc, mod_c)
            xc = _peer(xc, mod_c, *peer_args, final_norm=False)
    return xl
```
